```python
import math
import jax
import jax.numpy as jnp
from jax import lax
import numpy as np


D_MODEL = 1024
BATCH = 8
SEQ = 4096
DEPTH = 4

SSM_WIDTH = D_MODEL // 2
SSM_GROUP_WIDTH = 16
N_SSM_GROUPS = SSM_WIDTH // SSM_GROUP_WIDTH
SSM_STATE = 64
HEAD_DIM = 64
N_Q_HEADS = (D_MODEL - SSM_WIDTH) // HEAD_DIM
N_KV_HEADS = 2
GQA_GROUP = N_Q_HEADS // N_KV_HEADS
ATTN_WIDTH = N_Q_HEADS * HEAD_DIM
KV_WIDTH = N_KV_HEADS * HEAD_DIM
IN_WIDTH = SSM_WIDTH + ATTN_WIDTH + 2 * KV_WIDTH
MIX_WIDTH = SSM_WIDTH + ATTN_WIDTH
WINDOW = 128
BLOCK = 128
FFN_HIDDEN = int(math.ceil(8 * D_MODEL / 3 / 256) * 256)
PLE_DIM = 256
RMS_EPS = 1e-6
DT_MIN = 1e-3
DT_MAX = 1e-1

kernel_name = "hymba_s5_swa_sink_hybrid"


def rms_norm(x, g):
    xf = x.astype(jnp.float32)
    y = xf * lax.rsqrt(jnp.mean(xf * xf, axis=-1, keepdims=True) + RMS_EPS)
    return (y * g.astype(jnp.float32)).astype(x.dtype)


def s5_mixer(u, a_re, a_im, log_dt, b_re, b_im, c_re, c_im, d_skip, w_glu):
    bsz, seq, _ = u.shape
    f32 = jnp.float32
    uf = u.astype(f32).reshape(bsz, seq, N_SSM_GROUPS, SSM_GROUP_WIDTH)
    ar = a_re.astype(f32)
    ai = a_im.astype(f32)
    dt = jnp.exp(log_dt.astype(f32))[:, None]
    mag = jnp.exp(ar * dt)
    lb_re = mag * jnp.cos(ai * dt)
    lb_im = mag * jnp.sin(ai * dt)
    den = ar * ar + ai * ai
    nr = lb_re - 1.0
    ni = lb_im
    f_re = (nr * ar + ni * ai) / den
    f_im = (ni * ar - nr * ai) / den
    br = b_re.astype(f32)
    bi = b_im.astype(f32)
    bb_re = f_re[..., None] * br - f_im[..., None] * bi
    bb_im = f_re[..., None] * bi + f_im[..., None] * br
    bu_re = jnp.einsum('blgh,gph->blgp', uf, bb_re)
    bu_im = jnp.einsum('blgh,gph->blgp', uf, bb_im)
    a_t_re = jnp.broadcast_to(lb_re[None, None], (1, seq, N_SSM_GROUPS, SSM_STATE))
    a_t_im = jnp.broadcast_to(lb_im[None, None], (1, seq, N_SSM_GROUPS, SSM_STATE))

    def combine(e1, e2):
        a1r, a1i, b1r, b1i = e1
        a2r, a2i, b2r, b2i = e2
        return (a2r * a1r - a2i * a1i,
                a2r * a1i + a2i * a1r,
                a2r * b1r - a2i * b1i + b2r,
                a2r * b1i + a2i * b1r + b2i)

    _, _, xr, xi = lax.associative_scan(combine, (a_t_re, a_t_im, bu_re, bu_im), axis=1)
    y = (jnp.einsum('blgp,ghp->blgh', xr, c_re.astype(f32))
         - jnp.einsum('blgp,ghp->blgh', xi, c_im.astype(f32)))
    y = y.reshape(bsz, seq, SSM_WIDTH) + d_skip.astype(f32) * uf.reshape(bsz, seq, SSM_WIDTH)
    y = jax.nn.gelu(y)
    y = y * jax.nn.sigmoid(y @ w_glu.astype(f32))
    return y.astype(u.dtype)


def sliding_window_attention(q, k, v, sinks):
    bsz, seq, _, dh = q.shape
    nb = seq // BLOCK
    qb = q.reshape(bsz, nb, BLOCK, N_KV_HEADS, GQA_GROUP, dh)
    kb = k.reshape(bsz, nb, BLOCK, N_KV_HEADS, dh)
    vb = v.reshape(bsz, nb, BLOCK, N_KV_HEADS, dh)
    pad = ((0, 0), (1, 0), (0, 0), (0, 0), (0, 0))
    k_prev = jnp.pad(kb, pad)[:, :-1]
    v_prev = jnp.pad(vb, pad)[:, :-1]
    keys = jnp.concatenate([k_prev, kb], axis=2)
    vals = jnp.concatenate([v_prev, vb], axis=2)
    scale = 1.0 / math.sqrt(dh)
    scores = jnp.einsum('bnqhgd,bnkhd->bnhgqk', qb, keys).astype(jnp.float32) * scale
    qpos = jnp.arange(BLOCK)[:, None]
    kpos = jnp.arange(2 * BLOCK)[None, :] - BLOCK
    rel = qpos - kpos
    band = (rel >= 0) & (rel < WINDOW)
    blk = jnp.arange(nb)[:, None, None]
    valid = band[None] & ((blk > 0) | (kpos[None] >= 0))
    scores = jnp.where(valid[None, :, None, None], scores, -jnp.inf)
    sink = sinks.astype(jnp.float32).reshape(N_KV_HEADS, GQA_GROUP)[None, None, :, :, None, None]
    sink = jnp.broadcast_to(sink, scores.shape[:-1] + (1,))
    probs = jax.nn.softmax(jnp.concatenate([scores, sink], axis=-1), axis=-1)[..., :-1]
    out = jnp.einsum('bnhgqk,bnkhd->bnqhgd', probs.astype(v.dtype), vals)
    return out.reshape(bsz, seq, N_Q_HEADS * dh)


def swiglu(h, w_in, w_out):
    gu = h @ w_in
    gate, up = jnp.split(gu, 2, axis=-1)
    return (jax.nn.silu(gate) * up) @ w_out


def setup_inputs(seed: int = 0) -> dict:
    key = jax.random.key(seed)
    ks = jax.random.split(key, 26)
    f32 = jnp.float32

    def nrm(k, shape, scale):
        return jax.random.normal(k, shape, f32) * scale

    def gain(k, shape):
        return 1.0 + 0.02 * jax.random.normal(k, shape, f32)

    G, P, H = N_SSM_GROUPS, SSM_STATE, SSM_GROUP_WIDTH
    n_idx = jnp.arange(P, dtype=f32)
    ssm_a_re = -0.5 + 0.01 * jax.random.normal(ks[3], (DEPTH, G, P), f32)
    ssm_a_im = math.pi * n_idx[None, None, :] + 0.01 * jax.random.normal(ks[4], (DEPTH, G, P), f32)
    ssm_log_dt = jax.random.uniform(ks[5], (DEPTH, G), f32,
                                    math.log(DT_MIN), math.log(DT_MAX))
    return {
        "x": nrm(ks[0], (BATCH, SEQ, D_MODEL), 1.0),
        "p": nrm(ks[1], (DEPTH, BATCH, SEQ, PLE_DIM), 1.0),
        "norm_mix": gain(ks[2], (DEPTH, D_MODEL)),
        "w_in": nrm(ks[6], (DEPTH, D_MODEL, IN_WIDTH), D_MODEL ** -0.5),
        "ssm_a_re": ssm_a_re,
        "ssm_a_im": ssm_a_im,
        "ssm_log_dt": ssm_log_dt,
        "ssm_b_re": nrm(ks[7], (DEPTH, G, P, H), (2 * H) ** -0.5),
        "ssm_b_im": nrm(ks[8], (DEPTH, G, P, H), (2 * H) ** -0.5),
        "ssm_c_re": nrm(ks[9], (DEPTH, G, H, P), (2 * P) ** -0.5),
        "ssm_c_im": nrm(ks[10], (DEPTH, G, H, P), (2 * P) ** -0.5),
        "ssm_d": nrm(ks[11], (DEPTH, SSM_WIDTH), 1.0),
        "ssm_w_glu": nrm(ks[12], (DEPTH, SSM_WIDTH, SSM_WIDTH), SSM_WIDTH ** -0.5),
        "attn_sinks": nrm(ks[13], (DEPTH, N_Q_HEADS), 0.5),
        "norm_ssm_out": gain(ks[14], (DEPTH, SSM_WIDTH)),
        "norm_attn_out": gain(ks[15], (DEPTH, ATTN_WIDTH)),
        "w_out": nrm(ks[16], (DEPTH, MIX_WIDTH, D_MODEL), MIX_WIDTH ** -0.5),
        "norm_ffn": gain(ks[17], (DEPTH, D_MODEL)),
        "w_ffn_in": nrm(ks[18], (DEPTH, D_MODEL, 2 * FFN_HIDDEN), D_MODEL ** -0.5),
        "w_ffn_out": nrm(ks[19], (DEPTH, FFN_HIDDEN, D_MODEL), FFN_HIDDEN ** -0.5),
        "norm_ple": gain(ks[20], (DEPTH, D_MODEL)),
        "w_ple_gate": nrm(ks[21], (DEPTH, D_MODEL, D_MODEL), D_MODEL ** -0.5),
        "w_ple_proj": nrm(ks[22], (DEPTH, PLE_DIM, D_MODEL), PLE_DIM ** -0.5),
        "norm_final": gain(ks[23], (D_MODEL,)),
    }


def reference(x, p, norm_mix, w_in, ssm_a_re, ssm_a_im, ssm_log_dt, ssm_b_re, ssm_b_im,
              ssm_c_re, ssm_c_im, ssm_d, ssm_w_glu, attn_sinks, norm_ssm_out, norm_attn_out,
              w_out, norm_ffn, w_ffn_in, w_ffn_out, norm_ple, w_ple_gate, w_ple_proj,
              norm_final):
    bsz, seq, _ = x.shape
    h = x
    splits = [SSM_WIDTH, SSM_WIDTH + ATTN_WIDTH, SSM_WIDTH + ATTN_WIDTH + KV_WIDTH]
    for i in range(DEPTH):
        hn = rms_norm(h, norm_mix[i])
        proj = hn @ w_in[i]
        u, q, k, v = jnp.split(proj, splits, axis=-1)
        ssm_out = s5_mixer(u, ssm_a_re[i], ssm_a_im[i], ssm_log_dt[i], ssm_b_re[i], ssm_b_im[i],
                           ssm_c_re[i], ssm_c_im[i], ssm_d[i], ssm_w_glu[i])
        attn_out = sliding_window_attention(
            q.reshape(bsz, seq, N_Q_HEADS, HEAD_DIM),
            k.reshape(bsz, seq, N_KV_HEADS, HEAD_DIM),
            v.reshape(bsz, seq, N_KV_HEADS, HEAD_DIM),
            attn_sinks[i])
        mixed = jnp.concatenate([rms_norm(ssm_out, norm_ssm_out[i]),
                                 rms_norm(attn_out, norm_attn_out[i])], axis=-1)
        h = h + mixed @ w_out[i]
        h = h + swiglu(rms_norm(h, norm_ffn[i]), w_ffn_in[i], w_ffn_out[i])
        gate = jax.nn.sigmoid(rms_norm(h, norm_ple[i]) @ w_ple_gate[i])
        h = h + gate * (p[i] @ w_ple_proj[i])
    return rms_norm(h, norm_final)
```

```python
import functools
import math

import numpy as np
import jax
import jax.numpy as jnp
from jax import lax
from jax.experimental import pallas as pl
from jax.experimental.pallas import tpu as pltpu

D_MODEL = 1024
SSM_WIDTH = 512
SSM_GROUP_WIDTH = 16
N_SSM_GROUPS = 32
SSM_STATE = 64
HEAD_DIM = 64
N_Q_HEADS = 8
N_KV_HEADS = 2
ATTN_WIDTH = 512
KV_WIDTH = 128
IN_WIDTH = 1280
WINDOW = 128
BLOCK = 128
FFN_HIDDEN = 2816
PLE_DIM = 256
RMS_EPS = 1e-6

SUBLANES = 8
LANES = 128
MXU_DIM = 256

CLUSTER_GROUPS = MXU_DIM // SSM_GROUP_WIDTH
N_CLUSTERS = N_SSM_GROUPS // CLUSTER_GROUPS
CLUSTER_IN = CLUSTER_GROUPS * SSM_GROUP_WIDTH
CLUSTER_STATE = CLUSTER_GROUPS * SSM_STATE

TOKEN_TILE = 512
ATTN_TILE = 512
S5_CHUNK = 64
FFN_CHUNK = 256
VMEM_LIMIT = 60 * 1024 * 1024

_F32 = jnp.float32
_BF16 = jnp.bfloat16


def _dot(a, b):
    return jnp.dot(a, b, preferred_element_type=_F32)


def _rms(x, g):
    ms = jnp.mean(x * x, axis=-1, keepdims=True)
    return x * lax.rsqrt(ms + RMS_EPS) * g


def _sigmoid(x):
    return 1.0 / (1.0 + jnp.exp(-x))


def _gelu_tanh(x):
    c = math.sqrt(2.0 / math.pi)
    return 0.5 * x * (1.0 + jnp.tanh(c * (x + 0.044715 * (x * x * x))))


def _s5_prep_kernel(ar_ref, ai_ref, ldt_ref, cr_ref, ci_ref, lbr_ref, lbi_ref, cfr_ref, cfi_ref):
    ar = ar_ref[...]
    ai = ai_ref[...]
    dt = jnp.exp(ldt_ref[...])
    mag = jnp.exp(ar * dt)
    lb_re = mag * jnp.cos(ai * dt)
    lb_im = mag * jnp.sin(ai * dt)
    den = ar * ar + ai * ai
    nr = lb_re - 1.0
    ni = lb_im
    f_re = (nr * ar + ni * ai) / den
    f_im = (ni * ar - nr * ai) / den
    lbr_ref[...] = lb_re
    lbi_ref[...] = lb_im
    cr = cr_ref[...]
    ci = ci_ref[...]
    fr = f_re[:, None, :]
    fi = f_im[:, None, :]
    cfr_ref[...] = cr * fr - ci * fi
    cfi_ref[...] = cr * fi + ci * fr


def _s5_prep(a_re, a_im, log_dt, c_re, c_im):
    depth = a_re.shape[0]
    G, P, H = N_SSM_GROUPS, SSM_STATE, SSM_GROUP_WIDTH
    gp = pl.BlockSpec((None, G, P), lambda i: (i, 0, 0))
    g1 = pl.BlockSpec((None, G, 1), lambda i: (i, 0, 0))
    ghp = pl.BlockSpec((None, G, H, P), lambda i: (i, 0, 0, 0))
    return pl.pallas_call(
        _s5_prep_kernel,
        grid=(depth,),
        in_specs=[gp, gp, g1, ghp, ghp],
        out_specs=[gp, gp, ghp, ghp],
        out_shape=[jax.ShapeDtypeStruct((depth, G, P), _F32)] * 2
        + [jax.ShapeDtypeStruct((depth, G, H, P), _F32)] * 2,
        name="s5_prep",
    )(a_re, a_im, log_dt.reshape(depth, G, 1), c_re, c_im)


def _s5_block_diag(lb_re, lb_im, b_re, b_im, cf_re, cf_im):
    depth = lb_re.shape[0]
    C, CG, P, H = N_CLUSTERS, CLUSTER_GROUPS, SSM_STATE, SSM_GROUP_WIDTH
    eye = jnp.eye(CG, dtype=_F32)

    def b_side(b):
        b = b.reshape(depth, C, CG, P, H)
        return jnp.einsum('dcgph,gk->dcghkp', b, eye).reshape(depth, C, CG * H, CG * P)

    def c_side(c):
        c = c.reshape(depth, C, CG, H, P)
        return jnp.einsum('dcghp,gk->dcgpkh', c, eye).reshape(depth, C, CG * P, CG * H)

    bbd = jnp.concatenate([b_side(b_re), b_side(b_im)], axis=-1).astype(_BF16)
    cbd = jnp.concatenate([c_side(cf_re), -c_side(cf_im)], axis=-2).astype(_BF16)
    lam = jnp.stack([lb_re.reshape(depth, C, CG * P), lb_im.reshape(depth, C, CG * P)], axis=2)
    return lam, bbd, cbd


def _s5_kernel(u_ref, lam_ref, bbd_ref, cbd_ref, d_ref, wglu_ref, g_ref, o_ref, bu_ref, st_ref,
               *, chunk, batch):
    @pl.when(pl.program_id(0) == 0)
    def _():
        st_ref[...] = jnp.zeros_like(st_ref)

    u = u_ref[...]
    ys = []
    for c in range(N_CLUSTERS):
        bu_ref[...] = _dot(u[:, c * CLUSTER_IN:(c + 1) * CLUSTER_IN], bbd_ref[c])
        lr = jnp.broadcast_to(lam_ref[c, 0:1, :], (batch, CLUSTER_STATE))
        li = jnp.broadcast_to(lam_ref[c, 1:2, :], (batch, CLUSTER_STATE))

        def step(t, carry):
            sr, si = carry
            r0 = pl.multiple_of(t * batch, batch)
            br = bu_ref[pl.ds(r0, batch), 0:CLUSTER_STATE]
            bi = bu_ref[pl.ds(r0, batch), CLUSTER_STATE:2 * CLUSTER_STATE]
            nr = lr * sr - li * si + br
            ni = lr * si + li * sr + bi
            bu_ref[pl.ds(r0, batch), 0:CLUSTER_STATE] = nr
            bu_ref[pl.ds(r0, batch), CLUSTER_STATE:2 * CLUSTER_STATE] = ni
            return nr, ni

        sr, si = lax.fori_loop(0, chunk, step, (st_ref[c, 0], st_ref[c, 1]), unroll=2)
        st_ref[c, 0] = sr
        st_ref[c, 1] = si
        ys.append(_dot(bu_ref[...].astype(_BF16), cbd_ref[c]))
    y = jnp.concatenate(ys, axis=1) + d_ref[...] * u.astype(_F32)
    y = _gelu_tanh(y)
    y = y * _sigmoid(_dot(y.astype(_BF16), wglu_ref[...]))
    o_ref[...] = _rms(y, g_ref[...]).astype(o_ref.dtype)


def _s5_mixer(u2d, lam, bbd, cbd, d_skip, w_glu, g_ssm, layer, *, batch, seq):
    chunk = S5_CHUNK
    rows = chunk * batch
    const = lambda shape: pl.BlockSpec((None,) + shape, lambda i: (layer,) + (0,) * len(shape))
    return pl.pallas_call(
        functools.partial(_s5_kernel, chunk=chunk, batch=batch),
        grid=(seq // chunk,),
        in_specs=[
            pl.BlockSpec((rows, SSM_WIDTH), lambda i: (i, 0)),
            const((N_CLUSTERS, 2, CLUSTER_STATE)),
            const((N_CLUSTERS, CLUSTER_IN, 2 * CLUSTER_STATE)),
            const((N_CLUSTERS, 2 * CLUSTER_STATE, CLUSTER_IN)),
            const((1, SSM_WIDTH)),
            const((SSM_WIDTH, SSM_WIDTH)),
            const((1, SSM_WIDTH)),
        ],
        out_specs=pl.BlockSpec((rows, SSM_WIDTH), lambda i: (i, 0)),
        out_shape=jax.ShapeDtypeStruct((seq * batch, SSM_WIDTH), _BF16),
        scratch_shapes=[
            pltpu.VMEM((rows, 2 * CLUSTER_STATE), _F32),
            pltpu.VMEM((N_CLUSTERS, 2, batch, CLUSTER_STATE), _F32),
        ],
        compiler_params=pltpu.CompilerParams(
            dimension_semantics=("arbitrary",), vmem_limit_bytes=VMEM_LIMIT),
        name="s5_mixer",
    )(u2d, lam, bbd, cbd, d_skip, w_glu, g_ssm)


def _attn_kernel(sink_ref, q_ref, kc_ref, kp_ref, vc_ref, vp_ref, g_ref, o_ref, *, n_blocks):
    first_tile = pl.program_id(1) == 0
    gq = N_Q_HEADS // N_KV_HEADS
    rows = gq * BLOCK
    qpos = lax.broadcasted_iota(jnp.int32, (rows, 2 * BLOCK), 0) % BLOCK
    kidx = lax.broadcasted_iota(jnp.int32, (rows, 2 * BLOCK), 1)
    rel = qpos - kidx + BLOCK
    band = (rel >= 0) & (rel < WINDOW)
    band_first = band & ((kidx >= BLOCK) | jnp.logical_not(first_tile))
    head_of_row = lax.broadcasted_iota(jnp.int32, (rows, 1), 0) // BLOCK
    low_kv = lax.broadcasted_iota(jnp.int32, (2 * BLOCK, KV_WIDTH), 1) < HEAD_DIM
    low_out = lax.broadcasted_iota(jnp.int32, (rows, LANES), 1) < HEAD_DIM
    g = g_ref[...]

    sink_cols = []
    for kv in range(N_KV_HEADS):
        col = jnp.full((rows, 1), sink_ref[0, kv * gq], _F32)
        for j in range(1, gq):
            col = jnp.where(head_of_row == j, sink_ref[0, kv * gq + j], col)
        sink_cols.append(col)

    for blk in range(n_blocks):
        lo, hi = blk * BLOCK, (blk + 1) * BLOCK
        if blk == 0:
            k_prev, v_prev, valid = kp_ref[...], vp_ref[...], band_first
        else:
            k_prev, v_prev, valid = kc_ref[lo - BLOCK:lo, :], vc_ref[lo - BLOCK:lo, :], band
        keys = jnp.concatenate([k_prev, kc_ref[lo:hi, :]], axis=0)
        vals = jnp.concatenate([v_prev, vc_ref[lo:hi, :]], axis=0)
        zero = jnp.zeros_like(keys)
        q_all = jnp.concatenate([q_ref[lo:hi, j * LANES:(j + 1) * LANES] for j in range(gq)], axis=0)
        acc = None
        invs = []
        for kv in range(N_KV_HEADS):
            sel = low_kv if kv == 0 else jnp.logical_not(low_kv)
            k_sel = jnp.where(sel, keys, zero)
            v_sel = jnp.where(sel, vals, zero)
            s = lax.dot_general(q_all, k_sel, (((1,), (1,)), ((), ())),
                                preferred_element_type=_F32)
            s = jnp.where(valid, s, -jnp.inf)
            sink = sink_cols[kv]
            m = jnp.maximum(jnp.max(s, axis=1, keepdims=True), sink)
            e = jnp.exp(s - m)
            den = jnp.sum(e, axis=1, keepdims=True) + jnp.exp(sink - m)
            invs.append(1.0 / den)
            pv = _dot(e.astype(_BF16), v_sel)
            acc = pv if acc is None else acc + pv
        out = acc * jnp.where(low_out, invs[0], invs[1])
        outs = [out[j * BLOCK:(j + 1) * BLOCK, :] for j in range(gq)]
        ms = sum(jnp.sum(o * o, axis=1, keepdims=True) for o in outs) * (1.0 / ATTN_WIDTH)
        scale = lax.rsqrt(ms + RMS_EPS)
        for j in range(gq):
            o_ref[lo:hi, j * LANES:(j + 1) * LANES] = (
                outs[j] * scale * g[:, j * LANES:(j + 1) * LANES]).astype(o_ref.dtype)


def _attention(q_tm, k_tm, v_tm, sinks, g_attn, layer, *, batch, seq):
    tile = ATTN_TILE
    n_blocks = tile // BLOCK
    cur = lambda w: pl.BlockSpec((tile, w), lambda b, i: (i, b))
    prev = lambda w: pl.BlockSpec((BLOCK, w), lambda b, i: (jnp.maximum(i * n_blocks - 1, 0), b))
    return pl.pallas_call(
        functools.partial(_attn_kernel, n_blocks=n_blocks),
        grid=(batch, seq // tile),
        in_specs=[
            pl.BlockSpec((None, 1, N_Q_HEADS), lambda b, i: (layer, 0, 0),
                         memory_space=pltpu.SMEM),
            cur(ATTN_WIDTH), cur(KV_WIDTH), prev(KV_WIDTH), cur(KV_WIDTH), prev(KV_WIDTH),
            pl.BlockSpec((None, 1, ATTN_WIDTH), lambda b, i: (layer, 0, 0)),
        ],
        out_specs=cur(ATTN_WIDTH),
        out_shape=jax.ShapeDtypeStruct((seq, batch * ATTN_WIDTH), _BF16),
        compiler_params=pltpu.CompilerParams(
            dimension_semantics=("parallel", "parallel"), vmem_limit_bytes=VMEM_LIMIT),
        name="swa_attention",
    )(sinks, q_tm, k_tm, k_tm, v_tm, v_tm, g_attn)


def _in_proj(h, g_mix, w_in_ref, u_ref, q_ref, k_ref, v_ref):
    proj = _dot(_rms(h, g_mix).astype(_BF16), w_in_ref[...])
    u_ref[...] = proj[:, :SSM_WIDTH].astype(u_ref.dtype)
    q_ref[...] = (proj[:, SSM_WIDTH:SSM_WIDTH + ATTN_WIDTH] * (HEAD_DIM ** -0.5)).astype(q_ref.dtype)
    k_ref[...] = proj[:, SSM_WIDTH + ATTN_WIDTH:SSM_WIDTH + ATTN_WIDTH + KV_WIDTH].astype(k_ref.dtype)
    v_ref[...] = proj[:, SSM_WIDTH + ATTN_WIDTH + KV_WIDTH:].astype(v_ref.dtype)


def _pre_kernel(h_ref, g_mix_ref, w_in_ref, u_ref, q_ref, k_ref, v_ref):
    _in_proj(h_ref[...], g_mix_ref[...], w_in_ref, u_ref, q_ref, k_ref, v_ref)


def _mid_kernel(*refs, last):
    (h_ref, ssm_ref, attn_ref, p_ref, w_out_ref, g_ffn_ref, w_fi_ref, w_fo_ref, g_ple_ref,
     w_pg_ref, w_pp_ref) = refs[:11]
    if last:
        g_fin_ref, out_ref, acc_ref = refs[11:]
    else:
        g_mix_ref, w_in_ref, h_out_ref, u_ref, q_ref, k_ref, v_ref, acc_ref = refs[11:]

    h = (h_ref[...] + _dot(ssm_ref[...], w_out_ref[:SSM_WIDTH, :])
         + _dot(attn_ref[...], w_out_ref[SSM_WIDTH:, :]))
    hn = _rms(h, g_ffn_ref[...]).astype(_BF16)
    for c in range(FFN_HIDDEN // FFN_CHUNK):
        lo = c * FFN_CHUNK
        gate = _dot(hn, w_fi_ref[:, lo:lo + FFN_CHUNK])
        up = _dot(hn, w_fi_ref[:, FFN_HIDDEN + lo:FFN_HIDDEN + lo + FFN_CHUNK])
        act = (gate * _sigmoid(gate) * up).astype(_BF16)
        part = _dot(act, w_fo_ref[lo:lo + FFN_CHUNK, :])
        if c == 0:
            acc_ref[...] = part
        else:
            acc_ref[...] += part
    h = h + acc_ref[...]
    gate = _sigmoid(_dot(_rms(h, g_ple_ref[...]).astype(_BF16), w_pg_ref[...]))
    h = h + gate * _dot(p_ref[...].astype(_BF16), w_pp_ref[...])
    if last:
        out_ref[...] = _rms(h, g_fin_ref[...])
    else:
        h_out_ref[...] = h
        _in_proj(h, g_mix_ref[...], w_in_ref, u_ref, q_ref, k_ref, v_ref)


def _layer_const(layer, shape):
    return pl.BlockSpec((None,) + shape, lambda b, i: (layer,) + (0,) * len(shape),
                        pipeline_mode=pl.Buffered(1))


def _proj_out_specs(tile, batch, seq):
    tm = lambda w: pl.BlockSpec((tile, w), lambda b, i: (i, b))
    specs = [tm(SSM_WIDTH), tm(ATTN_WIDTH), tm(KV_WIDTH), tm(KV_WIDTH)]
    shapes = [jax.ShapeDtypeStruct((seq, batch * w), _BF16)
              for w in (SSM_WIDTH, ATTN_WIDTH, KV_WIDTH, KV_WIDTH)]
    return specs, shapes


def _pre(x2d, g_mix, w_in, *, batch, seq):
    tile = TOKEN_TILE
    n_t = seq // tile
    specs, shapes = _proj_out_specs(tile, batch, seq)
    return pl.pallas_call(
        _pre_kernel,
        grid=(batch, n_t),
        in_specs=[
            pl.BlockSpec((tile, D_MODEL), lambda b, i: (b * n_t + i, 0)),
            _layer_const(0, (1, D_MODEL)),
            _layer_const(0, (D_MODEL, IN_WIDTH)),
        ],
        out_specs=specs,
        out_shape=shapes,
        compiler_params=pltpu.CompilerParams(
            dimension_semantics=("parallel", "parallel"), vmem_limit_bytes=VMEM_LIMIT),
        name="pre_in_proj",
    )(x2d, g_mix, w_in)


def _mid(h2d, ssm_tm, attn_tm, p2d, weights, layer, *, batch, seq, depth, h_batch_major):
    tile = TOKEN_TILE
    n_t = seq // tile
    last = layer == depth - 1
    bm = lambda w, off=0: pl.BlockSpec((tile, w), lambda b, i: (off + b * n_t + i, 0))
    tm = lambda w: pl.BlockSpec((tile, w), lambda b, i: (i, b))
    in_specs = [
        bm(D_MODEL) if h_batch_major else tm(D_MODEL),
        tm(SSM_WIDTH), tm(ATTN_WIDTH),
        bm(PLE_DIM, layer * batch * n_t),
        _layer_const(layer, (D_MODEL, D_MODEL)),
        _layer_const(layer, (1, D_MODEL)),
        _layer_const(layer, (D_MODEL, 2 * FFN_HIDDEN)),
        _layer_const(layer, (FFN_HIDDEN, D_MODEL)),
        _layer_const(layer, (1, D_MODEL)),
        _layer_const(layer, (D_MODEL, D_MODEL)),
        _layer_const(layer, (PLE_DIM, D_MODEL)),
    ]
    args = [h2d, ssm_tm, attn_tm, p2d, weights["w_out"], weights["g_ffn"], weights["w_fi"],
            weights["w_fo"], weights["g_ple"], weights["w_pg"], weights["w_pp"]]
    if last:
        in_specs.append(pl.BlockSpec((1, D_MODEL), lambda b, i: (0, 0)))
        args.append(weights["g_fin"])
        out_specs = [bm(D_MODEL)]
        out_shape = [jax.ShapeDtypeStruct((batch * seq, D_MODEL), _F32)]
    else:
        in_specs += [_layer_const(layer + 1, (1, D_MODEL)),
                     _layer_const(layer + 1, (D_MODEL, IN_WIDTH))]
        args += [weights["g_mix"], weights["w_in"]]
        specs, shapes = _proj_out_specs(tile, batch, seq)
        out_specs = [tm(D_MODEL)] + specs
        out_shape = [jax.ShapeDtypeStruct((seq, batch * D_MODEL), _F32)] + shapes
    return pl.pallas_call(
        functools.partial(_mid_kernel, last=last),
        grid=(batch, n_t),
        in_specs=in_specs,
        out_specs=out_specs,
        out_shape=out_shape,
        scratch_shapes=[pltpu.VMEM((tile, D_MODEL), _F32)],
        compiler_params=pltpu.CompilerParams(
            dimension_semantics=("parallel", "parallel"), vmem_limit_bytes=VMEM_LIMIT),
        name="mid_last" if last else "mid",
    )(*args)


def _q_column_order():
    gq = N_Q_HEADS // N_KV_HEADS
    cols = []
    for j in range(gq):
        for kv in range(N_KV_HEADS):
            head = kv * gq + j
            cols.extend(range(head * HEAD_DIM, (head + 1) * HEAD_DIM))
    return np.asarray(cols, dtype=np.int32)


def kernel(x, p, norm_mix, w_in, ssm_a_re, ssm_a_im, ssm_log_dt, ssm_b_re, ssm_b_im, ssm_c_re,
           ssm_c_im, ssm_d, ssm_w_glu, attn_sinks, norm_ssm_out, norm_attn_out, w_out, norm_ffn,
           w_ffn_in, w_ffn_out, norm_ple, w_ple_gate, w_ple_proj, norm_final):
    batch, seq, _ = x.shape
    depth = w_in.shape[0]
    assert batch == SUBLANES and seq % max(TOKEN_TILE, ATTN_TILE, S5_CHUNK) == 0

    perm = _q_column_order()
    in_cols = np.concatenate([np.arange(SSM_WIDTH), SSM_WIDTH + perm,
                              np.arange(SSM_WIDTH + ATTN_WIDTH, IN_WIDTH)])
    out_rows = np.concatenate([np.arange(SSM_WIDTH), SSM_WIDTH + perm])
    row = lambda a: a.reshape(a.shape[0], 1, a.shape[-1])
    weights = {
        "g_mix": row(norm_mix),
        "w_in": w_in[:, :, in_cols].astype(_BF16),
        "w_out": w_out[:, out_rows, :].astype(_BF16),
        "g_ffn": row(norm_ffn),
        "w_fi": w_ffn_in.astype(_BF16),
        "w_fo": w_ffn_out.astype(_BF16),
        "g_ple": row(norm_ple),
        "w_pg": w_ple_gate.astype(_BF16),
        "w_pp": w_ple_proj.astype(_BF16),
        "g_fin": norm_final.reshape(1, D_MODEL),
    }
    g_attn = row(norm_attn_out[:, perm])
    g_ssm = row(norm_ssm_out)
    d_skip = row(ssm_d)
    w_glu = ssm_w_glu.astype(_BF16)
    sinks = row(attn_sinks)

    lb_re, lb_im, cf_re, cf_im = _s5_prep(ssm_a_re, ssm_a_im, ssm_log_dt, ssm_c_re, ssm_c_im)
    lam, bbd, cbd = _s5_block_diag(lb_re, lb_im, ssm_b_re, ssm_b_im, cf_re, cf_im)

    p2d = p.reshape(depth * batch * seq, PLE_DIM)
    h2d = x.reshape(batch * seq, D_MODEL)
    u, q, k, v = _pre(h2d, weights["g_mix"], weights["w_in"], batch=batch, seq=seq)
    for layer in range(depth):
        ssm = _s5_mixer(u.reshape(seq * batch, SSM_WIDTH), lam, bbd, cbd, d_skip, w_glu, g_ssm,
                        layer, batch=batch, seq=seq)
        attn = _attention(q, k, v, sinks, g_attn, layer, batch=batch, seq=seq)
        outs = _mid(h2d, ssm.reshape(seq, batch * SSM_WIDTH), attn, p2d, weights, layer,
                    batch=batch, seq=seq, depth=depth, h_batch_major=(layer == 0))
        if layer == depth - 1:
            return outs[0].reshape(batch, seq, D_MODEL)
        h2d, u, q, k, v = outs
```

```python
import functools
import math

import numpy as np
import jax
import jax.numpy as jnp
from jax import lax
from jax.experimental import pallas as pl
from jax.experimental.pallas import tpu as pltpu

D_MODEL = 1024
SSM_WIDTH = 512
SSM_GROUP_WIDTH = 16
N_SSM_GROUPS = 32
SSM_STATE = 64
HEAD_DIM = 64
N_Q_HEADS = 8
N_KV_HEADS = 2
ATTN_WIDTH = 512
KV_WIDTH = 128
IN_WIDTH = 1280
WINDOW = 128
BLOCK = 128
FFN_HIDDEN = 2816
PLE_DIM = 256
RMS_EPS = 1e-6

SUBLANES = 8
LANES = 128
MXU_DIM = 256

CLUSTER_GROUPS = MXU_DIM // SSM_GROUP_WIDTH
N_CLUSTERS = N_SSM_GROUPS // CLUSTER_GROUPS
CLUSTER_IN = CLUSTER_GROUPS * SSM_GROUP_WIDTH
CLUSTER_STATE = CLUSTER_GROUPS * SSM_STATE

TOKEN_TILE = 512
ATTN_TILE = 512
S5_CHUNK = 64
FFN_CHUNK = 256
VMEM_LIMIT = 60 * 1024 * 1024

_F32 = jnp.float32
_BF16 = jnp.bfloat16


def _dot(a, b):
    return jnp.dot(a, b, preferred_element_type=_F32)


def _rms(x, g):
    ms = jnp.mean(x * x, axis=-1, keepdims=True)
    return x * lax.rsqrt(ms + RMS_EPS) * g


def _sigmoid(x):
    return 1.0 / (1.0 + jnp.exp(-x))


def _gelu_tanh(x):
    c = math.sqrt(2.0 / math.pi)
    return 0.5 * x * (1.0 + jnp.tanh(c * (x + 0.044715 * (x * x * x))))


def _s5_prep_kernel(ar_ref, ai_ref, ldt_ref, cr_ref, ci_ref, lbr_ref, lbi_ref, cfr_ref, cfi_ref):
    ar = ar_ref[...]
    ai = ai_ref[...]
    dt = jnp.exp(ldt_ref[...])
    mag = jnp.exp(ar * dt)
    lb_re = mag * jnp.cos(ai * dt)
    lb_im = mag * jnp.sin(ai * dt)
    den = ar * ar + ai * ai
    nr = lb_re - 1.0
    ni = lb_im
    f_re = (nr * ar + ni * ai) / den
    f_im = (ni * ar - nr * ai) / den
    lbr_ref[...] = lb_re
    lbi_ref[...] = lb_im
    cr = cr_ref[...]
    ci = ci_ref[...]
    fr = f_re[:, None, :]
    fi = f_im[:, None, :]
    cfr_ref[...] = cr * fr - ci * fi
    cfi_ref[...] = cr * fi + ci * fr


def _s5_prep(a_re, a_im, log_dt, c_re, c_im):
    depth = a_re.shape[0]
    G, P, H = N_SSM_GROUPS, SSM_STATE, SSM_GROUP_WIDTH
    gp = pl.BlockSpec((None, G, P), lambda i: (i, 0, 0))
    g1 = pl.BlockSpec((None, G, 1), lambda i: (i, 0, 0))
    ghp = pl.BlockSpec((None, G, H, P), lambda i: (i, 0, 0, 0))
    return pl.pallas_call(
        _s5_prep_kernel,
        grid=(depth,),
        in_specs=[gp, gp, g1, ghp, ghp],
        out_specs=[gp, gp, ghp, ghp],
        out_shape=[jax.ShapeDtypeStruct((depth, G, P), _F32)] * 2
        + [jax.ShapeDtypeStruct((depth, G, H, P), _F32)] * 2,
        name="s5_prep",
    )(a_re, a_im, log_dt.reshape(depth, G, 1), c_re, c_im)


def _s5_block_diag(lb_re, lb_im, b_re, b_im, cf_re, cf_im):
    depth = lb_re.shape[0]
    C, CG, P, H = N_CLUSTERS, CLUSTER_GROUPS, SSM_STATE, SSM_GROUP_WIDTH
    eye = jnp.eye(CG, dtype=_F32)

    def b_side(b):
        b = b.reshape(depth, C, CG, P, H)
        return jnp.einsum('dcgph,gk->dcghkp', b, eye).reshape(depth, C, CG * H, CG * P)

    def c_side(c):
        c = c.reshape(depth, C, CG, H, P)
        return jnp.einsum('dcghp,gk->dcgpkh', c, eye).reshape(depth, C, CG * P, CG * H)

    bbd = jnp.concatenate([b_side(b_re), b_side(b_im)], axis=-1).astype(_BF16)
    cbd = jnp.concatenate([c_side(cf_re), -c_side(cf_im)], axis=-2).astype(_BF16)
    lam = jnp.stack([lb_re.reshape(depth, C, CG * P), lb_im.reshape(depth, C, CG * P)], axis=2)
    return lam, bbd, cbd


def _s5_kernel(u_ref, lam_ref, bbd_ref, cbd_ref, d_ref, wglu_ref, g_ref, o_ref,
               us_ref, ys_ref, bu_ref, st_ref, *, chunk, batch):
    n_slab = SSM_WIDTH // LANES
    slabs_per_cluster = CLUSTER_IN // LANES

    @pl.when(pl.program_id(0) == 0)
    def _():
        st_ref[...] = jnp.zeros_like(st_ref)

    for b in range(batch):
        for j in range(n_slab):
            lo = b * SSM_WIDTH + j * LANES
            us_ref[j, pl.ds(b, chunk, stride=batch), :] = u_ref[:, lo:lo + LANES].astype(_F32)

    ys = []
    for c in range(N_CLUSTERS):
        u_c = jnp.concatenate(
            [us_ref[c * slabs_per_cluster + j] for j in range(slabs_per_cluster)], axis=1)
        bu_ref[c] = _dot(u_c.astype(_BF16), bbd_ref[c])
        lr = jnp.broadcast_to(lam_ref[c, 0:1, :], (batch, CLUSTER_STATE))
        li = jnp.broadcast_to(lam_ref[c, 1:2, :], (batch, CLUSTER_STATE))
        sr, si = st_ref[c, 0], st_ref[c, 1]
        for t in range(chunk):
            r0 = t * batch
            br = bu_ref[c, r0:r0 + batch, 0:CLUSTER_STATE]
            bi = bu_ref[c, r0:r0 + batch, CLUSTER_STATE:2 * CLUSTER_STATE]
            sr, si = lr * sr - li * si + br, lr * si + li * sr + bi
            bu_ref[c, r0:r0 + batch, 0:CLUSTER_STATE] = sr
            bu_ref[c, r0:r0 + batch, CLUSTER_STATE:2 * CLUSTER_STATE] = si
        st_ref[c, 0] = sr
        st_ref[c, 1] = si
        ys.append(_dot(bu_ref[c].astype(_BF16), cbd_ref[c]))
    u_all = jnp.concatenate([us_ref[j] for j in range(n_slab)], axis=1)
    y = jnp.concatenate(ys, axis=1) + d_ref[...] * u_all
    y = _gelu_tanh(y)
    y = y * _sigmoid(_dot(y.astype(_BF16), wglu_ref[...]))
    y = _rms(y, g_ref[...])
    for j in range(n_slab):
        ys_ref[j] = y[:, j * LANES:(j + 1) * LANES]
    for b in range(batch):
        for j in range(n_slab):
            lo = b * SSM_WIDTH + j * LANES
            o_ref[:, lo:lo + LANES] = ys_ref[j, pl.ds(b, chunk, stride=batch), :].astype(o_ref.dtype)


def _s5_mixer(u_tm, lam, bbd, cbd, d_skip, w_glu, g_ssm, layer, *, batch, seq):
    chunk = S5_CHUNK
    rows = chunk * batch
    n_slab = SSM_WIDTH // LANES
    const = lambda shape: pl.BlockSpec((None,) + shape, lambda i: (layer,) + (0,) * len(shape))
    return pl.pallas_call(
        functools.partial(_s5_kernel, chunk=chunk, batch=batch),
        grid=(seq // chunk,),
        in_specs=[
            pl.BlockSpec((chunk, batch * SSM_WIDTH), lambda i: (i, 0)),
            const((N_CLUSTERS, 2, CLUSTER_STATE)),
            const((N_CLUSTERS, CLUSTER_IN, 2 * CLUSTER_STATE)),
            const((N_CLUSTERS, 2 * CLUSTER_STATE, CLUSTER_IN)),
            const((1, SSM_WIDTH)),
            const((SSM_WIDTH, SSM_WIDTH)),
            const((1, SSM_WIDTH)),
        ],
        out_specs=pl.BlockSpec((chunk, batch * SSM_WIDTH), lambda i: (i, 0)),
        out_shape=jax.ShapeDtypeStruct((seq, batch * SSM_WIDTH), _BF16),
        scratch_shapes=[
            pltpu.VMEM((n_slab, rows, LANES), _F32),
            pltpu.VMEM((n_slab, rows, LANES), _F32),
            pltpu.VMEM((N_CLUSTERS, rows, 2 * CLUSTER_STATE), _F32),
            pltpu.VMEM((N_CLUSTERS, 2, batch, CLUSTER_STATE), _F32),
        ],
        compiler_params=pltpu.CompilerParams(
            dimension_semantics=("arbitrary",), vmem_limit_bytes=VMEM_LIMIT),
        name="s5_mixer",
    )(u_tm, lam, bbd, cbd, d_skip, w_glu, g_ssm)


def _attn_kernel(sink_ref, q_ref, kc_ref, kp_ref, vc_ref, vp_ref, g_ref, o_ref, *, n_blocks):
    first_tile = pl.program_id(1) == 0
    gq = N_Q_HEADS // N_KV_HEADS
    rows = gq * BLOCK
    qpos = lax.broadcasted_iota(jnp.int32, (rows, BLOCK), 0) % BLOCK
    kidx = lax.broadcasted_iota(jnp.int32, (rows, BLOCK), 1)
    own = kidx <= qpos
    head_of_row = lax.broadcasted_iota(jnp.int32, (rows, 1), 0) // BLOCK
    low_kv = lax.broadcasted_iota(jnp.int32, (2 * BLOCK, KV_WIDTH), 1) < HEAD_DIM
    low_out = lax.broadcasted_iota(jnp.int32, (rows, LANES), 1) < HEAD_DIM
    g = g_ref[...]

    sink_cols = []
    for kv in range(N_KV_HEADS):
        col = jnp.full((rows, 1), sink_ref[0, kv * gq], _F32)
        for j in range(1, gq):
            col = jnp.where(head_of_row == j, sink_ref[0, kv * gq + j], col)
        sink_cols.append(col)

    for blk in range(n_blocks):
        lo, hi = blk * BLOCK, (blk + 1) * BLOCK
        if blk == 0:
            k_prev, v_prev = kp_ref[...], vp_ref[...]
        else:
            k_prev, v_prev = kc_ref[lo - BLOCK:lo, :], vc_ref[lo - BLOCK:lo, :]
        keys = jnp.concatenate([k_prev, kc_ref[lo:hi, :]], axis=0)
        vals = jnp.concatenate([v_prev, vc_ref[lo:hi, :]], axis=0)
        zero = jnp.zeros_like(keys)
        q_all = jnp.concatenate([q_ref[lo:hi, j * LANES:(j + 1) * LANES] for j in range(gq)], axis=0)
        acc = None
        invs = []
        for kv in range(N_KV_HEADS):
            sel = low_kv if kv == 0 else jnp.logical_not(low_kv)
            k_sel = jnp.where(sel, keys, zero)
            v_sel = jnp.where(sel, vals, zero)
            s = lax.dot_general(q_all, k_sel, (((1,), (1,)), ((), ())),
                                preferred_element_type=_F32)
            s_prev = s[:, :BLOCK]
            if blk == 0:
                s_prev = jnp.where(first_tile, -jnp.inf, s_prev)
            s = jnp.where(own, s[:, BLOCK:], s_prev)
            sink = sink_cols[kv]
            m = jnp.maximum(jnp.max(s, axis=1, keepdims=True), sink)
            e = jnp.exp(s - m)
            den = jnp.sum(e, axis=1, keepdims=True) + jnp.exp(sink - m)
            invs.append(1.0 / den)
            p = jnp.concatenate([jnp.where(own, 0.0, e), jnp.where(own, e, 0.0)], axis=1)
            pv = _dot(p.astype(_BF16), v_sel)
            acc = pv if acc is None else acc + pv
        out = acc * jnp.where(low_out, invs[0], invs[1])
        outs = [out[j * BLOCK:(j + 1) * BLOCK, :] for j in range(gq)]
        ms = sum(jnp.sum(o * o, axis=1, keepdims=True) for o in outs) * (1.0 / ATTN_WIDTH)
        scale = lax.rsqrt(ms + RMS_EPS)
        for j in range(gq):
            o_ref[lo:hi, j * LANES:(j + 1) * LANES] = (
                outs[j] * scale * g[:, j * LANES:(j + 1) * LANES]).astype(o_ref.dtype)


def _attention(q_tm, k_tm, v_tm, sinks, g_attn, layer, *, batch, seq):
    tile = ATTN_TILE
    n_blocks = tile // BLOCK
    cur = lambda w: pl.BlockSpec((tile, w), lambda b, i: (i, b))
    prev = lambda w: pl.BlockSpec((BLOCK, w), lambda b, i: (jnp.maximum(i * n_blocks - 1, 0), b))
    return pl.pallas_call(
        functools.partial(_attn_kernel, n_blocks=n_blocks),
        grid=(batch, seq // tile),
        in_specs=[
            pl.BlockSpec((None, 1, N_Q_HEADS), lambda b, i: (layer, 0, 0),
                         memory_space=pltpu.SMEM),
            cur(ATTN_WIDTH), cur(KV_WIDTH), prev(KV_WIDTH), cur(KV_WIDTH), prev(KV_WIDTH),
            pl.BlockSpec((None, 1, ATTN_WIDTH), lambda b, i: (layer, 0, 0)),
        ],
        out_specs=cur(ATTN_WIDTH),
        out_shape=jax.ShapeDtypeStruct((seq, batch * ATTN_WIDTH), _BF16),
        compiler_params=pltpu.CompilerParams(
            dimension_semantics=("parallel", "parallel"), vmem_limit_bytes=VMEM_LIMIT),
        name="swa_attention",
    )(sinks, q_tm, k_tm, k_tm, v_tm, v_tm, g_attn)


def _in_proj(h, g_mix, w_in_ref, u_ref, q_ref, k_ref, v_ref):
    proj = _dot(_rms(h, g_mix).astype(_BF16), w_in_ref[...])
    u_ref[...] = proj[:, :SSM_WIDTH].astype(u_ref.dtype)
    q_ref[...] = (proj[:, SSM_WIDTH:SSM_WIDTH + ATTN_WIDTH] * (HEAD_DIM ** -0.5)).astype(q_ref.dtype)
    k_ref[...] = proj[:, SSM_WIDTH + ATTN_WIDTH:SSM_WIDTH + ATTN_WIDTH + KV_WIDTH].astype(k_ref.dtype)
    v_ref[...] = proj[:, SSM_WIDTH + ATTN_WIDTH + KV_WIDTH:].astype(v_ref.dtype)


def _pre_kernel(h_ref, g_mix_ref, w_in_ref, u_ref, q_ref, k_ref, v_ref):
    _in_proj(h_ref[...], g_mix_ref[...], w_in_ref, u_ref, q_ref, k_ref, v_ref)


def _mid_kernel(*refs, last):
    (h_ref, ssm_ref, attn_ref, p_ref, w_out_ref, g_ffn_ref, w_fi_ref, w_fo_ref, g_ple_ref,
     w_pg_ref, w_pp_ref) = refs[:11]
    if last:
        g_fin_ref, out_ref, acc_ref = refs[11:]
    else:
        g_mix_ref, w_in_ref, h_out_ref, u_ref, q_ref, k_ref, v_ref, acc_ref = refs[11:]

    h = (h_ref[...] + _dot(ssm_ref[...], w_out_ref[:SSM_WIDTH, :])
         + _dot(attn_ref[...], w_out_ref[SSM_WIDTH:, :]))
    hn = _rms(h, g_ffn_ref[...]).astype(_BF16)
    for c in range(FFN_HIDDEN // FFN_CHUNK):
        lo = c * FFN_CHUNK
        gate = _dot(hn, w_fi_ref[:, lo:lo + FFN_CHUNK])
        up = _dot(hn, w_fi_ref[:, FFN_HIDDEN + lo:FFN_HIDDEN + lo + FFN_CHUNK])
        act = (gate * _sigmoid(gate) * up).astype(_BF16)
        part = _dot(act, w_fo_ref[lo:lo + FFN_CHUNK, :])
        if c == 0:
            acc_ref[...] = part
        else:
            acc_ref[...] += part
    h = h + acc_ref[...]
    gate = _sigmoid(_dot(_rms(h, g_ple_ref[...]).astype(_BF16), w_pg_ref[...]))
    h = h + gate * _dot(p_ref[...].astype(_BF16), w_pp_ref[...])
    if last:
        out_ref[...] = _rms(h, g_fin_ref[...])
    else:
        h_out_ref[...] = h
        _in_proj(h, g_mix_ref[...], w_in_ref, u_ref, q_ref, k_ref, v_ref)


def _layer_const(layer, shape):
    return pl.BlockSpec((None,) + shape, lambda b, i: (layer,) + (0,) * len(shape),
                        pipeline_mode=pl.Buffered(1))


def _proj_out_specs(tile, batch, seq):
    tm = lambda w: pl.BlockSpec((tile, w), lambda b, i: (i, b))
    specs = [tm(SSM_WIDTH), tm(ATTN_WIDTH), tm(KV_WIDTH), tm(KV_WIDTH)]
    shapes = [jax.ShapeDtypeStruct((seq, batch * w), _BF16)
              for w in (SSM_WIDTH, ATTN_WIDTH, KV_WIDTH, KV_WIDTH)]
    return specs, shapes


def _pre(x2d, g_mix, w_in, *, batch, seq):
    tile = TOKEN_TILE
    n_t = seq // tile
    specs, shapes = _proj_out_specs(tile, batch, seq)
    return pl.pallas_call(
        _pre_kernel,
        grid=(batch, n_t),
        in_specs=[
            pl.BlockSpec((tile, D_MODEL), lambda b, i: (b * n_t + i, 0)),
            _layer_const(0, (1, D_MODEL)),
            _layer_const(0, (D_MODEL, IN_WIDTH)),
        ],
        out_specs=specs,
        out_shape=shapes,
        compiler_params=pltpu.CompilerParams(
            dimension_semantics=("parallel", "parallel"), vmem_limit_bytes=VMEM_LIMIT),
        name="pre_in_proj",
    )(x2d, g_mix, w_in)


def _mid(h2d, ssm_tm, attn_tm, p2d, weights, layer, *, batch, seq, depth, h_batch_major):
    tile = TOKEN_TILE
    n_t = seq // tile
    last = layer == depth - 1
    bm = lambda w, off=0: pl.BlockSpec((tile, w), lambda b, i: (off + b * n_t + i, 0))
    tm = lambda w: pl.BlockSpec((tile, w), lambda b, i: (i, b))
    in_specs = [
        bm(D_MODEL) if h_batch_major else tm(D_MODEL),
        tm(SSM_WIDTH), tm(ATTN_WIDTH),
        bm(PLE_DIM, layer * batch * n_t),
        _layer_const(layer, (D_MODEL, D_MODEL)),
        _layer_const(layer, (1, D_MODEL)),
        _layer_const(layer, (D_MODEL, 2 * FFN_HIDDEN)),
        _layer_const(layer, (FFN_HIDDEN, D_MODEL)),
        _layer_const(layer, (1, D_MODEL)),
        _layer_const(layer, (D_MODEL, D_MODEL)),
        _layer_const(layer, (PLE_DIM, D_MODEL)),
    ]
    args = [h2d, ssm_tm, attn_tm, p2d, weights["w_out"], weights["g_ffn"], weights["w_fi"],
            weights["w_fo"], weights["g_ple"], weights["w_pg"], weights["w_pp"]]
    if last:
        in_specs.append(pl.BlockSpec((1, D_MODEL), lambda b, i: (0, 0)))
        args.append(weights["g_fin"])
        out_specs = [bm(D_MODEL)]
        out_shape = [jax.ShapeDtypeStruct((batch * seq, D_MODEL), _F32)]
    else:
        in_specs += [_layer_const(layer + 1, (1, D_MODEL)),
                     _layer_const(layer + 1, (D_MODEL, IN_WIDTH))]
        args += [weights["g_mix"], weights["w_in"]]
        specs, shapes = _proj_out_specs(tile, batch, seq)
        out_specs = [tm(D_MODEL)] + specs
        out_shape = [jax.ShapeDtypeStruct((seq, batch * D_MODEL), _F32)] + shapes
    return pl.pallas_call(
        functools.partial(_mid_kernel, last=last),
        grid=(batch, n_t),
        in_specs=in_specs,
        out_specs=out_specs,
        out_shape=out_shape,
        scratch_shapes=[pltpu.VMEM((tile, D_MODEL), _F32)],
        compiler_params=pltpu.CompilerParams(
            dimension_semantics=("parallel", "parallel"), vmem_limit_bytes=VMEM_LIMIT),
        name="mid_last" if last else "mid",
    )(*args)


def _q_column_order():
    gq = N_Q_HEADS // N_KV_HEADS
    cols = []
    for j in range(gq):
        for kv in range(N_KV_HEADS):
            head = kv * gq + j
            cols.extend(range(head * HEAD_DIM, (head + 1) * HEAD_DIM))
    return np.asarray(cols, dtype=np.int32)


def kernel(x, p, norm_mix, w_in, ssm_a_re, ssm_a_im, ssm_log_dt, ssm_b_re, ssm_b_im, ssm_c_re,
           ssm_c_im, ssm_d, ssm_w_glu, attn_sinks, norm_ssm_out, norm_attn_out, w_out, norm_ffn,
           w_ffn_in, w_ffn_out, norm_ple, w_ple_gate, w_ple_proj, norm_final):
    batch, seq, _ = x.shape
    depth = w_in.shape[0]
    assert batch == SUBLANES and seq % max(TOKEN_TILE, ATTN_TILE, S5_CHUNK) == 0

    perm = _q_column_order()
    in_cols = np.concatenate([np.arange(SSM_WIDTH), SSM_WIDTH + perm,
                              np.arange(SSM_WIDTH + ATTN_WIDTH, IN_WIDTH)])
    out_rows = np.concatenate([np.arange(SSM_WIDTH), SSM_WIDTH + perm])
    row = lambda a: a.reshape(a.shape[0], 1, a.shape[-1])
    weights = {
        "g_mix": row(norm_mix),
        "w_in": w_in[:, :, in_cols].astype(_BF16),
        "w_out": w_out[:, out_rows, :].astype(_BF16),
        "g_ffn": row(norm_ffn),
        "w_fi": w_ffn_in.astype(_BF16),
        "w_fo": w_ffn_out.astype(_BF16),
        "g_ple": row(norm_ple),
        "w_pg": w_ple_gate.astype(_BF16),
        "w_pp": w_ple_proj.astype(_BF16),
        "g_fin": norm_final.reshape(1, D_MODEL),
    }
    g_attn = row(norm_attn_out[:, perm])
    g_ssm = row(norm_ssm_out)
    d_skip = row(ssm_d)
    w_glu = ssm_w_glu.astype(_BF16)
    sinks = row(attn_sinks)

    lb_re, lb_im, cf_re, cf_im = _s5_prep(ssm_a_re, ssm_a_im, ssm_log_dt, ssm_c_re, ssm_c_im)
    lam, bbd, cbd = _s5_block_diag(lb_re, lb_im, ssm_b_re, ssm_b_im, cf_re, cf_im)

    p2d = p.reshape(depth * batch * seq, PLE_DIM)
    h2d = x.reshape(batch * seq, D_MODEL)
    u, q, k, v = _pre(h2d, weights["g_mix"], weights["w_in"], batch=batch, seq=seq)
    for layer in range(depth):
        ssm = _s5_mixer(u, lam, bbd, cbd, d_skip, w_glu, g_ssm, layer, batch=batch, seq=seq)
        attn = _attention(q, k, v, sinks, g_attn, layer, batch=batch, seq=seq)
        outs = _mid(h2d, ssm, attn, p2d, weights, layer,
                    batch=batch, seq=seq, depth=depth, h_batch_major=(layer == 0))
        if layer == depth - 1:
            return outs[0].reshape(batch, seq, D_MODEL)
        h2d, u, q, k, v = outs
```

```python
import functools
import math

import jax
import jax.numpy as jnp
from jax import lax
from jax.experimental import pallas as pl
from jax.experimental.pallas import tpu as pltpu

D_MODEL = 1024
SSM_WIDTH = 512
SSM_GROUP_WIDTH = 16
N_SSM_GROUPS = 32
SSM_STATE = 64
HEAD_DIM = 64
N_Q_HEADS = 8
N_KV_HEADS = 2
ATTN_WIDTH = 512
KV_WIDTH = 128
IN_WIDTH = 1280
WINDOW = 128
BLOCK = 128
FFN_HIDDEN = 2816
PLE_DIM = 256
RMS_EPS = 1e-6

SUBLANES = 8
LANES = 128
MXU_DIM = 256

CLUSTER_GROUPS = MXU_DIM // SSM_GROUP_WIDTH
N_CLUSTERS = N_SSM_GROUPS // CLUSTER_GROUPS
CLUSTER_IN = CLUSTER_GROUPS * SSM_GROUP_WIDTH
CLUSTER_STATE = CLUSTER_GROUPS * SSM_STATE

TOKEN_TILE = 512
S5_CHUNK = 64
FFN_CHUNK = 256
VMEM_LIMIT = 60 * 1024 * 1024
LOG2_E = math.log2(math.e)
Q_SCALE = HEAD_DIM ** -0.5 * LOG2_E

_F32 = jnp.float32
_BF16 = jnp.bfloat16


def _dot(a, b):
    return jnp.dot(a, b, preferred_element_type=_F32)


def _rms(x, g):
    ms = jnp.mean(x * x, axis=-1, keepdims=True)
    return x * lax.rsqrt(ms + RMS_EPS) * g


def _sigmoid(x):
    return 1.0 / (1.0 + jnp.exp(-x))


def _gelu_tanh(x):
    c = math.sqrt(2.0 / math.pi)
    return 0.5 * x * (1.0 + jnp.tanh(c * (x + 0.044715 * (x * x * x))))


def _s5_prep_kernel(ar_ref, ai_ref, ldt_ref, cr_ref, ci_ref, lbr_ref, lbi_ref, cfr_ref, cfi_ref):
    ar = ar_ref[...]
    ai = ai_ref[...]
    dt = jnp.exp(ldt_ref[...])
    mag = jnp.exp(ar * dt)
    lb_re = mag * jnp.cos(ai * dt)
    lb_im = mag * jnp.sin(ai * dt)
    den = ar * ar + ai * ai
    nr = lb_re - 1.0
    ni = lb_im
    f_re = (nr * ar + ni * ai) / den
    f_im = (ni * ar - nr * ai) / den
    lbr_ref[...] = lb_re
    lbi_ref[...] = lb_im
    cr = cr_ref[...]
    ci = ci_ref[...]
    fr = f_re[:, None, :]
    fi = f_im[:, None, :]
    cfr_ref[...] = cr * fr - ci * fi
    cfi_ref[...] = cr * fi + ci * fr


def _s5_prep(a_re, a_im, log_dt, c_re, c_im):
    depth = a_re.shape[0]
    G, P, H = N_SSM_GROUPS, SSM_STATE, SSM_GROUP_WIDTH
    gp = pl.BlockSpec((None, G, P), lambda i: (i, 0, 0))
    g1 = pl.BlockSpec((None, G, 1), lambda i: (i, 0, 0))
    ghp = pl.BlockSpec((None, G, H, P), lambda i: (i, 0, 0, 0))
    return pl.pallas_call(
        _s5_prep_kernel,
        grid=(depth,),
        in_specs=[gp, gp, g1, ghp, ghp],
        out_specs=[gp, gp, ghp, ghp],
        out_shape=[jax.ShapeDtypeStruct((depth, G, P), _F32)] * 2
        + [jax.ShapeDtypeStruct((depth, G, H, P), _F32)] * 2,
        name="s5_prep",
    )(a_re, a_im, log_dt.reshape(depth, G, 1), c_re, c_im)


def _s5_block_diag(lb_re, lb_im, b_re, b_im, cf_re, cf_im):
    depth = lb_re.shape[0]
    C, CG, P, H = N_CLUSTERS, CLUSTER_GROUPS, SSM_STATE, SSM_GROUP_WIDTH
    eye = jnp.eye(CG, dtype=_F32)

    def b_side(b):
        b = b.reshape(depth, C, CG, P, H)
        return jnp.einsum('dcgph,gk->dcghkp', b, eye).reshape(depth, C, CG * H, CG * P)

    def c_side(c):
        c = c.reshape(depth, C, CG, H, P)
        return jnp.einsum('dcghp,gk->dcgpkh', c, eye).reshape(depth, C, CG * P, CG * H)

    bbd = jnp.concatenate([b_side(b_re), b_side(b_im)], axis=-1).astype(_BF16)
    cbd = jnp.concatenate([c_side(cf_re), -c_side(cf_im)], axis=-2).astype(_BF16)
    lam = jnp.stack([lb_re.reshape(depth, C, CG * P), lb_im.reshape(depth, C, CG * P)], axis=2)
    return lam, bbd, cbd


SCAN_SLICES = 4


def _s5_phases(first_step, u_ref, lam_ref, bbd_ref, cbd_ref, d_ref, wglu_ref, g_ref, o_ref,
               us_ref, ys_ref, bu_ref, st_ref, *, chunk, batch):
    n_slab = SSM_WIDTH // LANES
    slabs_per_cluster = CLUSTER_IN // LANES

    @pl.when(first_step)
    def _():
        st_ref[...] = jnp.zeros_like(st_ref)

    for b in range(batch):
        for j in range(n_slab):
            lo = b * SSM_WIDTH + j * LANES
            us_ref[j, pl.ds(b, chunk, stride=batch), :] = u_ref[:, lo:lo + LANES].astype(_F32)
    for c in range(N_CLUSTERS):
        u_c = jnp.concatenate(
            [us_ref[c * slabs_per_cluster + j] for j in range(slabs_per_cluster)], axis=1)
        bu_ref[c] = _dot(u_c.astype(_BF16), bbd_ref[c])
    yield

    ys = []
    for c in range(N_CLUSTERS):
        lr = jnp.broadcast_to(lam_ref[c, 0:1, :], (batch, CLUSTER_STATE))
        li = jnp.broadcast_to(lam_ref[c, 1:2, :], (batch, CLUSTER_STATE))
        sr, si = st_ref[c, 0], st_ref[c, 1]
        for t in range(chunk):
            r0 = t * batch
            br = bu_ref[c, r0:r0 + batch, 0:CLUSTER_STATE]
            bi = bu_ref[c, r0:r0 + batch, CLUSTER_STATE:2 * CLUSTER_STATE]
            sr, si = lr * sr - li * si + br, lr * si + li * sr + bi
            bu_ref[c, r0:r0 + batch, 0:CLUSTER_STATE] = sr
            bu_ref[c, r0:r0 + batch, CLUSTER_STATE:2 * CLUSTER_STATE] = si
            if (t + 1) % (chunk // SCAN_SLICES) == 0:
                yield
        st_ref[c, 0] = sr
        st_ref[c, 1] = si
        ys.append(_dot(bu_ref[c].astype(_BF16), cbd_ref[c]))
        yield
    u_all = jnp.concatenate([us_ref[j] for j in range(n_slab)], axis=1)
    y = jnp.concatenate(ys, axis=1) + d_ref[...] * u_all
    y = _gelu_tanh(y)
    z = _dot(y.astype(_BF16), wglu_ref[...])
    yield
    y = _rms(y * _sigmoid(z), g_ref[...])
    for j in range(n_slab):
        ys_ref[j] = y[:, j * LANES:(j + 1) * LANES]
    for b in range(batch):
        for j in range(n_slab):
            lo = b * SSM_WIDTH + j * LANES
            o_ref[:, lo:lo + LANES] = ys_ref[j, pl.ds(b, chunk, stride=batch), :].astype(o_ref.dtype)


def _attn_phases(first_tile, sink_ref, q_ref, kc_ref, kp_ref, vc_ref, vp_ref, g_ref, o_ref, *,
                 n_blocks):
    gq = N_Q_HEADS // N_KV_HEADS
    rows = gq * BLOCK
    qpos = lax.broadcasted_iota(jnp.int32, (rows, BLOCK), 0) % BLOCK
    kidx = lax.broadcasted_iota(jnp.int32, (rows, BLOCK), 1)
    own = kidx <= qpos
    head_of_row = lax.broadcasted_iota(jnp.int32, (rows, 1), 0) // BLOCK
    low_kv = lax.broadcasted_iota(jnp.int32, (2 * BLOCK, KV_WIDTH), 1) < HEAD_DIM
    low_out = lax.broadcasted_iota(jnp.int32, (rows, LANES), 1) < HEAD_DIM
    g = g_ref[...]

    sink_cols = []
    for kv in range(N_KV_HEADS):
        col = jnp.full((rows, 1), sink_ref[0, kv * gq], _F32)
        for j in range(1, gq):
            col = jnp.where(head_of_row == j, sink_ref[0, kv * gq + j], col)
        sink_cols.append(col * LOG2_E)

    for blk in range(n_blocks):
        lo, hi = blk * BLOCK, (blk + 1) * BLOCK
        if blk == 0:
            k_prev, v_prev = kp_ref[...], vp_ref[...]
        else:
            k_prev, v_prev = kc_ref[lo - BLOCK:lo, :], vc_ref[lo - BLOCK:lo, :]
        keys = jnp.concatenate([k_prev, kc_ref[lo:hi, :]], axis=0)
        vals = jnp.concatenate([v_prev, vc_ref[lo:hi, :]], axis=0)
        zero = jnp.zeros_like(keys)
        q_all = jnp.concatenate([q_ref[lo:hi, j * LANES:(j + 1) * LANES] for j in range(gq)], axis=0)
        acc = None
        invs = []
        for kv in range(N_KV_HEADS):
            sel = low_kv if kv == 0 else jnp.logical_not(low_kv)
            k_sel = jnp.where(sel, keys, zero)
            v_sel = jnp.where(sel, vals, zero)
            s = lax.dot_general(q_all, k_sel, (((1,), (1,)), ((), ())),
                                preferred_element_type=_F32)
            s_prev = s[:, :BLOCK]
            if blk == 0:
                s_prev = jnp.where(first_tile, -jnp.inf, s_prev)
            s = jnp.where(own, s[:, BLOCK:], s_prev)
            sink = sink_cols[kv]
            m = jnp.maximum(jnp.max(s, axis=1, keepdims=True), sink)
            e = jnp.exp2(s - m)
            den = jnp.sum(e, axis=1, keepdims=True) + jnp.exp2(sink - m)
            invs.append(1.0 / den)
            p = jnp.concatenate([jnp.where(own, 0.0, e), jnp.where(own, e, 0.0)], axis=1)
            pv = _dot(p.astype(_BF16), v_sel)
            acc = pv if acc is None else acc + pv
        out = acc * jnp.where(low_out, invs[0], invs[1])
        outs = [out[j * BLOCK:(j + 1) * BLOCK, :] for j in range(gq)]
        ms = sum(jnp.sum(o * o, axis=1, keepdims=True) for o in outs) * (1.0 / ATTN_WIDTH)
        scale = lax.rsqrt(ms + RMS_EPS)
        for j in range(gq):
            o_ref[lo:hi, j * LANES:(j + 1) * LANES] = (
                outs[j] * scale * g[:, j * LANES:(j + 1) * LANES]).astype(o_ref.dtype)
        yield


N_S5_IN = 7
N_ATTN_IN = 7
ATTN_AFTER_S5_PHASE = (0, 3, 6, 10)


def _mixer_kernel(*refs, chunk, batch, n_blocks, tiles_per_seq):
    s5_in = refs[:N_S5_IN]
    attn_in = refs[N_S5_IN:N_S5_IN + N_ATTN_IN]
    ssm_ref, attn_ref = refs[N_S5_IN + N_ATTN_IN:N_S5_IN + N_ATTN_IN + 2]
    scratch = refs[N_S5_IN + N_ATTN_IN + 2:]
    step = pl.program_id(0)
    s5 = _s5_phases(step == 0, *s5_in, ssm_ref, *scratch, chunk=chunk, batch=batch)
    attn = _attn_phases(step % tiles_per_seq == 0, *attn_in, attn_ref, n_blocks=n_blocks)
    for phase, _ in enumerate(s5):
        if phase in ATTN_AFTER_S5_PHASE:
            next(attn)
    for _ in attn:
        pass


def _mixer(u_tm, q_tm, k_tm, v_tm, s5w, sinks, g_attn, layer, *, batch, seq):
    chunk = S5_CHUNK
    rows = chunk * batch
    tile = rows
    n_blocks = tile // BLOCK
    tiles_per_seq = seq // tile
    n_slab = SSM_WIDTH // LANES
    const = lambda shape: pl.BlockSpec((None,) + shape, lambda i: (layer,) + (0,) * len(shape))
    cur = lambda w: pl.BlockSpec((tile, w), lambda i: (i % tiles_per_seq, i // tiles_per_seq))
    prev = lambda w: pl.BlockSpec(
        (BLOCK, w),
        lambda i: (jnp.maximum((i % tiles_per_seq) * n_blocks - 1, 0), i // tiles_per_seq))
    return pl.pallas_call(
        functools.partial(_mixer_kernel, chunk=chunk, batch=batch, n_blocks=n_blocks,
                          tiles_per_seq=tiles_per_seq),
        grid=(seq // chunk,),
        in_specs=[
            pl.BlockSpec((chunk, batch * SSM_WIDTH), lambda i: (i, 0)),
            const((N_CLUSTERS, 2, CLUSTER_STATE)),
            const((N_CLUSTERS, CLUSTER_IN, 2 * CLUSTER_STATE)),
            const((N_CLUSTERS, 2 * CLUSTER_STATE, CLUSTER_IN)),
            const((1, SSM_WIDTH)),
            const((SSM_WIDTH, SSM_WIDTH)),
            const((1, SSM_WIDTH)),
            pl.BlockSpec((None, 1, N_Q_HEADS), lambda i: (layer, 0, 0), memory_space=pltpu.SMEM),
            cur(ATTN_WIDTH), cur(KV_WIDTH), prev(KV_WIDTH), cur(KV_WIDTH), prev(KV_WIDTH),
            const((1, ATTN_WIDTH)),
        ],
        out_specs=[pl.BlockSpec((chunk, batch * SSM_WIDTH), lambda i: (i, 0)), cur(ATTN_WIDTH)],
        out_shape=[jax.ShapeDtypeStruct((seq, batch * SSM_WIDTH), _BF16),
                   jax.ShapeDtypeStruct((seq, batch * ATTN_WIDTH), _BF16)],
        scratch_shapes=[
            pltpu.VMEM((n_slab, rows, LANES), _F32),
            pltpu.VMEM((n_slab, rows, LANES), _F32),
            pltpu.VMEM((N_CLUSTERS, rows, 2 * CLUSTER_STATE), _F32),
            pltpu.VMEM((N_CLUSTERS, 2, batch, CLUSTER_STATE), _F32),
        ],
        compiler_params=pltpu.CompilerParams(
            dimension_semantics=("arbitrary",), vmem_limit_bytes=VMEM_LIMIT),
        name="mixer",
    )(u_tm, s5w["lam"], s5w["bbd"], s5w["cbd"], s5w["d_skip"], s5w["w_glu"], s5w["g_ssm"],
      sinks, q_tm, k_tm, k_tm, v_tm, v_tm, g_attn)


def _in_proj(h, g_mix, w_in_ref, u_ref, q_ref, k_ref, v_ref):
    proj = _dot(_rms(h, g_mix).astype(_BF16), w_in_ref[...])
    u_ref[...] = proj[:, :SSM_WIDTH].astype(u_ref.dtype)
    q_ref[...] = (proj[:, SSM_WIDTH:SSM_WIDTH + ATTN_WIDTH] * Q_SCALE).astype(q_ref.dtype)
    k_ref[...] = proj[:, SSM_WIDTH + ATTN_WIDTH:SSM_WIDTH + ATTN_WIDTH + KV_WIDTH].astype(k_ref.dtype)
    v_ref[...] = proj[:, SSM_WIDTH + ATTN_WIDTH + KV_WIDTH:].astype(v_ref.dtype)


def _pre_kernel(h_ref, g_mix_ref, w_in_ref, u_ref, q_ref, k_ref, v_ref):
    _in_proj(h_ref[...], g_mix_ref[...], w_in_ref, u_ref, q_ref, k_ref, v_ref)


def _mid_kernel(*refs, last):
    (h_ref, ssm_ref, attn_ref, p_ref, w_out_ref, g_ffn_ref, w_fi_ref, w_fo_ref, g_ple_ref,
     w_pg_ref, w_pp_ref) = refs[:11]
    if last:
        g_fin_ref, out_ref, acc_ref = refs[11:]
    else:
        g_mix_ref, w_in_ref, h_out_ref, u_ref, q_ref, k_ref, v_ref, acc_ref = refs[11:]

    h = (h_ref[...] + _dot(ssm_ref[...], w_out_ref[:SSM_WIDTH, :])
         + _dot(attn_ref[...], w_out_ref[SSM_WIDTH:, :]))
    hn = _rms(h, g_ffn_ref[...]).astype(_BF16)
    for c in range(FFN_HIDDEN // FFN_CHUNK):
        lo = c * FFN_CHUNK
        gate = _dot(hn, w_fi_ref[:, lo:lo + FFN_CHUNK])
        up = _dot(hn, w_fi_ref[:, FFN_HIDDEN + lo:FFN_HIDDEN + lo + FFN_CHUNK])
        act = (gate * _sigmoid(gate) * up).astype(_BF16)
        part = _dot(act, w_fo_ref[lo:lo + FFN_CHUNK, :])
        if c == 0:
            acc_ref[...] = part
        else:
            acc_ref[...] += part
    h = h + acc_ref[...]
    gate = _sigmoid(_dot(_rms(h, g_ple_ref[...]).astype(_BF16), w_pg_ref[...]))
    h = h + gate * _dot(p_ref[...].astype(_BF16), w_pp_ref[...])
    if last:
        out_ref[...] = _rms(h, g_fin_ref[...])
    else:
        h_out_ref[...] = h
        _in_proj(h, g_mix_ref[...], w_in_ref, u_ref, q_ref, k_ref, v_ref)


def _layer_const(layer, shape):
    return pl.BlockSpec((None,) + shape, lambda b, i: (layer,) + (0,) * len(shape),
                        pipeline_mode=pl.Buffered(1))


def _proj_out_specs(tile, batch, seq):
    tm = lambda w: pl.BlockSpec((tile, w), lambda b, i: (i, b))
    specs = [tm(SSM_WIDTH), tm(ATTN_WIDTH), tm(KV_WIDTH), tm(KV_WIDTH)]
    shapes = [jax.ShapeDtypeStruct((seq, batch * w), _BF16)
              for w in (SSM_WIDTH, ATTN_WIDTH, KV_WIDTH, KV_WIDTH)]
    return specs, shapes


def _pre(x2d, g_mix, w_in, *, batch, seq):
    tile = TOKEN_TILE
    n_t = seq // tile
    specs, shapes = _proj_out_specs(tile, batch, seq)
    return pl.pallas_call(
        _pre_kernel,
        grid=(batch, n_t),
        in_specs=[
            pl.BlockSpec((tile, D_MODEL), lambda b, i: (b * n_t + i, 0)),
            _layer_const(0, (1, D_MODEL)),
            _layer_const(0, (D_MODEL, IN_WIDTH)),
        ],
        out_specs=specs,
        out_shape=shapes,
        compiler_params=pltpu.CompilerParams(
            dimension_semantics=("parallel", "parallel"), vmem_limit_bytes=VMEM_LIMIT),
        name="pre_in_proj",
    )(x2d, g_mix, w_in)


def _mid(h2d, ssm_tm, attn_tm, p2d, weights, layer, *, batch, seq, depth, h_batch_major):
    tile = TOKEN_TILE
    n_t = seq // tile
    last = layer == depth - 1
    bm = lambda w, off=0: pl.BlockSpec((tile, w), lambda b, i: (off + b * n_t + i, 0))
    tm = lambda w: pl.BlockSpec((tile, w), lambda b, i: (i, b))
    in_specs = [
        bm(D_MODEL) if h_batch_major else tm(D_MODEL),
        tm(SSM_WIDTH), tm(ATTN_WIDTH),
        bm(PLE_DIM, layer * batch * n_t),
        _layer_const(layer, (D_MODEL, D_MODEL)),
        _layer_const(layer, (1, D_MODEL)),
        _layer_const(layer, (D_MODEL, 2 * FFN_HIDDEN)),
        _layer_const(layer, (FFN_HIDDEN, D_MODEL)),
        _layer_const(layer, (1, D_MODEL)),
        _layer_const(layer, (D_MODEL, D_MODEL)),
        _layer_const(layer, (PLE_DIM, D_MODEL)),
    ]
    args = [h2d, ssm_tm, attn_tm, p2d, weights["w_out"], weights["g_ffn"], weights["w_fi"],
            weights["w_fo"], weights["g_ple"], weights["w_pg"], weights["w_pp"]]
    if last:
        in_specs.append(pl.BlockSpec((1, D_MODEL), lambda b, i: (0, 0)))
        args.append(weights["g_fin"])
        out_specs = [bm(D_MODEL)]
        out_shape = [jax.ShapeDtypeStruct((batch * seq, D_MODEL), _F32)]
    else:
        in_specs += [_layer_const(layer + 1, (1, D_MODEL)),
                     _layer_const(layer + 1, (D_MODEL, IN_WIDTH))]
        args += [weights["g_mix"], weights["w_in"]]
        specs, shapes = _proj_out_specs(tile, batch, seq)
        out_specs = [tm(D_MODEL)] + specs
        out_shape = [jax.ShapeDtypeStruct((seq, batch * D_MODEL), _F32)] + shapes
    return pl.pallas_call(
        functools.partial(_mid_kernel, last=last),
        grid=(batch, n_t),
        in_specs=in_specs,
        out_specs=out_specs,
        out_shape=out_shape,
        scratch_shapes=[pltpu.VMEM((tile, D_MODEL), _F32)],
        compiler_params=pltpu.CompilerParams(
            dimension_semantics=("parallel", "parallel"), vmem_limit_bytes=VMEM_LIMIT),
        name="mid_last" if last else "mid",
    )(*args)


def _pair_heads(a, axis):
    gq = N_Q_HEADS // N_KV_HEADS
    shape = a.shape
    a = a.reshape(shape[:axis] + (N_KV_HEADS, gq, HEAD_DIM) + shape[axis + 1:])
    return jnp.swapaxes(a, axis, axis + 1).reshape(shape)


def kernel(x, p, norm_mix, w_in, ssm_a_re, ssm_a_im, ssm_log_dt, ssm_b_re, ssm_b_im, ssm_c_re,
           ssm_c_im, ssm_d, ssm_w_glu, attn_sinks, norm_ssm_out, norm_attn_out, w_out, norm_ffn,
           w_ffn_in, w_ffn_out, norm_ple, w_ple_gate, w_ple_proj, norm_final):
    batch, seq, _ = x.shape
    depth = w_in.shape[0]
    assert batch == SUBLANES and seq % max(TOKEN_TILE, S5_CHUNK * batch) == 0

    row = lambda a: a.reshape(a.shape[0], 1, a.shape[-1])
    q_lo, q_hi = SSM_WIDTH, SSM_WIDTH + ATTN_WIDTH
    weights = {
        "g_mix": row(norm_mix),
        "w_in": jnp.concatenate(
            [w_in[:, :, :q_lo], _pair_heads(w_in[:, :, q_lo:q_hi], 2), w_in[:, :, q_hi:]],
            axis=2).astype(_BF16),
        "w_out": jnp.concatenate(
            [w_out[:, :SSM_WIDTH, :], _pair_heads(w_out[:, SSM_WIDTH:, :], 1)],
            axis=1).astype(_BF16),
        "g_ffn": row(norm_ffn),
        "w_fi": w_ffn_in.astype(_BF16),
        "w_fo": w_ffn_out.astype(_BF16),
        "g_ple": row(norm_ple),
        "w_pg": w_ple_gate.astype(_BF16),
        "w_pp": w_ple_proj.astype(_BF16),
        "g_fin": norm_final.reshape(1, D_MODEL),
    }
    g_attn = row(_pair_heads(norm_attn_out, 1))
    sinks = row(attn_sinks)

    lb_re, lb_im, cf_re, cf_im = _s5_prep(ssm_a_re, ssm_a_im, ssm_log_dt, ssm_c_re, ssm_c_im)
    lam, bbd, cbd = _s5_block_diag(lb_re, lb_im, ssm_b_re, ssm_b_im, cf_re, cf_im)
    s5w = {"lam": lam, "bbd": bbd, "cbd": cbd, "d_skip": row(ssm_d),
           "w_glu": ssm_w_glu.astype(_BF16), "g_ssm": row(norm_ssm_out)}

    p2d = p.reshape(depth * batch * seq, PLE_DIM)
    h2d = x.reshape(batch * seq, D_MODEL)
    u, q, k, v = _pre(h2d, weights["g_mix"], weights["w_in"], batch=batch, seq=seq)
    for layer in range(depth):
        ssm, attn = _mixer(u, q, k, v, s5w, sinks, g_attn, layer, batch=batch, seq=seq)
        outs = _mid(h2d, ssm, attn, p2d, weights, layer,
                    batch=batch, seq=seq, depth=depth, h_batch_major=(layer == 0))
        if layer == depth - 1:
            return outs[0].reshape(batch, seq, D_MODEL)
        h2d, u, q, k, v = outs
```

```python
import functools
import math

import jax
import jax.numpy as jnp
from jax import lax
from jax.experimental import pallas as pl
from jax.experimental.pallas import tpu as pltpu

D_MODEL = 1024
SSM_WIDTH = 512
SSM_GROUP_WIDTH = 16
N_SSM_GROUPS = 32
SSM_STATE = 64
HEAD_DIM = 64
N_Q_HEADS = 8
N_KV_HEADS = 2
ATTN_WIDTH = 512
KV_WIDTH = 128
IN_WIDTH = 1280
WINDOW = 128
BLOCK = 128
FFN_HIDDEN = 2816
PLE_DIM = 256
RMS_EPS = 1e-6

SUBLANES = 8
LANES = 128
MXU_DIM = 256

CLUSTER_GROUPS = MXU_DIM // SSM_GROUP_WIDTH
N_CLUSTERS = N_SSM_GROUPS // CLUSTER_GROUPS
CLUSTER_IN = CLUSTER_GROUPS * SSM_GROUP_WIDTH
CLUSTER_STATE = CLUSTER_GROUPS * SSM_STATE

TOKEN_TILE = 512
S5_CHUNK = 64
FFN_CHUNK = 256
VMEM_LIMIT = 60 * 1024 * 1024
LOG2_E = math.log2(math.e)
Q_SCALE = HEAD_DIM ** -0.5 * LOG2_E

_F32 = jnp.float32
_BF16 = jnp.bfloat16


def _dot(a, b):
    return jnp.dot(a, b, preferred_element_type=_F32)


def _rms(x, g):
    ms = jnp.mean(x * x, axis=-1, keepdims=True)
    return x * lax.rsqrt(ms + RMS_EPS) * g


def _sigmoid(x):
    return 1.0 / (1.0 + jnp.exp(-x))


def _gelu_tanh(x):
    c = math.sqrt(2.0 / math.pi)
    return 0.5 * x * (1.0 + jnp.tanh(c * (x + 0.044715 * (x * x * x))))


def _s5_prep_kernel(ar_ref, ai_ref, ldt_ref, btr_ref, bti_ref, cr_ref, ci_ref,
                    lbr_ref, lbi_ref, bbd_ref, cbd_ref):
    G, H, P = N_SSM_GROUPS, SSM_GROUP_WIDTH, SSM_STATE
    ar = ar_ref[...]
    ai = ai_ref[...]
    dt = jnp.exp(ldt_ref[...])
    mag = jnp.exp(ar * dt)
    lb_re = mag * jnp.cos(ai * dt)
    lb_im = mag * jnp.sin(ai * dt)
    den = ar * ar + ai * ai
    nr = lb_re - 1.0
    ni = lb_im
    f_re = (nr * ar + ni * ai) / den
    f_im = (ni * ar - nr * ai) / den
    lbr_ref[...] = lb_re
    lbi_ref[...] = lb_im
    cr = cr_ref[...]
    ci = ci_ref[...]
    fr = f_re[:, None, :]
    fi = f_im[:, None, :]
    cf_re = (cr * fr - ci * fi).reshape(G * H, P)
    cf_im_neg = (-(cr * fi + ci * fr)).reshape(G * H, P)

    spread = (lax.broadcasted_iota(jnp.int32, (P, CLUSTER_STATE), 1) % P
              == lax.broadcasted_iota(jnp.int32, (P, CLUSTER_STATE), 0)).astype(_BF16)
    own_group = (lax.broadcasted_iota(jnp.int32, (CLUSTER_IN, CLUSTER_STATE), 0) // H
                 == lax.broadcasted_iota(jnp.int32, (CLUSTER_IN, CLUSTER_STATE), 1) // P)

    def block_diag(rows_gh_p):
        return jnp.where(own_group, _dot(rows_gh_p.astype(_BF16), spread), 0.0)

    for c in range(N_CLUSTERS):
        rows = slice(c * CLUSTER_IN, (c + 1) * CLUSTER_IN)
        for part, (bt_ref, cf) in enumerate(((btr_ref, cf_re), (bti_ref, cf_im_neg))):
            cols = slice(part * CLUSTER_STATE, (part + 1) * CLUSTER_STATE)
            bbd_ref[c, :, cols] = block_diag(bt_ref[rows, :]).astype(bbd_ref.dtype)
            cbd_ref[c, cols, :] = block_diag(cf[rows, :]).T.astype(cbd_ref.dtype)


def _s5_prep(a_re, a_im, log_dt, b_re, b_im, c_re, c_im):
    depth = a_re.shape[0]
    G, P, H, C = N_SSM_GROUPS, SSM_STATE, SSM_GROUP_WIDTH, N_CLUSTERS
    gp = pl.BlockSpec((None, G, P), lambda i: (i, 0, 0))
    g1 = pl.BlockSpec((None, G, 1), lambda i: (i, 0, 0))
    gh_p = pl.BlockSpec((None, G * H, P), lambda i: (i, 0, 0))
    ghp = pl.BlockSpec((None, G, H, P), lambda i: (i, 0, 0, 0))
    rows_gh = lambda b: jnp.swapaxes(b, 2, 3).reshape(depth, G * H, P)
    lb_re, lb_im, bbd, cbd = pl.pallas_call(
        _s5_prep_kernel,
        grid=(depth,),
        in_specs=[gp, gp, g1, gh_p, gh_p, ghp, ghp],
        out_specs=[gp, gp,
                   pl.BlockSpec((None, C, CLUSTER_IN, 2 * CLUSTER_STATE), lambda i: (i, 0, 0, 0)),
                   pl.BlockSpec((None, C, 2 * CLUSTER_STATE, CLUSTER_IN), lambda i: (i, 0, 0, 0))],
        out_shape=[jax.ShapeDtypeStruct((depth, G, P), _F32)] * 2
        + [jax.ShapeDtypeStruct((depth, C, CLUSTER_IN, 2 * CLUSTER_STATE), _BF16),
           jax.ShapeDtypeStruct((depth, C, 2 * CLUSTER_STATE, CLUSTER_IN), _BF16)],
        name="s5_prep",
    )(a_re, a_im, log_dt.reshape(depth, G, 1), rows_gh(b_re), rows_gh(b_im), c_re, c_im)
    lam = jnp.stack([lb_re.reshape(depth, C, CLUSTER_STATE), lb_im.reshape(depth, C, CLUSTER_STATE)],
                    axis=2)
    return lam, bbd, cbd


SCAN_SLICES = 4


def _s5_phases(first_step, u_ref, lam_ref, bbd_ref, cbd_ref, d_ref, wglu_ref, g_ref, o_ref,
               us_ref, ys_ref, bu_ref, st_ref, *, chunk, batch):
    n_slab = SSM_WIDTH // LANES
    slabs_per_cluster = CLUSTER_IN // LANES

    @pl.when(first_step)
    def _():
        st_ref[...] = jnp.zeros_like(st_ref)

    for b in range(batch):
        for j in range(n_slab):
            lo = b * SSM_WIDTH + j * LANES
            us_ref[j, pl.ds(b, chunk, stride=batch), :] = u_ref[:, lo:lo + LANES].astype(_F32)
    for c in range(N_CLUSTERS):
        u_c = jnp.concatenate(
            [us_ref[c * slabs_per_cluster + j] for j in range(slabs_per_cluster)], axis=1)
        bu_ref[c] = _dot(u_c.astype(_BF16), bbd_ref[c])
    yield

    ys = []
    for c in range(N_CLUSTERS):
        lr = jnp.broadcast_to(lam_ref[c, 0:1, :], (batch, CLUSTER_STATE))
        li = jnp.broadcast_to(lam_ref[c, 1:2, :], (batch, CLUSTER_STATE))
        sr, si = st_ref[c, 0], st_ref[c, 1]
        for t in range(chunk):
            r0 = t * batch
            br = bu_ref[c, r0:r0 + batch, 0:CLUSTER_STATE]
            bi = bu_ref[c, r0:r0 + batch, CLUSTER_STATE:2 * CLUSTER_STATE]
            sr, si = lr * sr - li * si + br, lr * si + li * sr + bi
            bu_ref[c, r0:r0 + batch, 0:CLUSTER_STATE] = sr
            bu_ref[c, r0:r0 + batch, CLUSTER_STATE:2 * CLUSTER_STATE] = si
            if (t + 1) % (chunk // SCAN_SLICES) == 0:
                yield
        st_ref[c, 0] = sr
        st_ref[c, 1] = si
        ys.append(_dot(bu_ref[c].astype(_BF16), cbd_ref[c]))
        yield
    u_all = jnp.concatenate([us_ref[j] for j in range(n_slab)], axis=1)
    y = jnp.concatenate(ys, axis=1) + d_ref[...] * u_all
    y = _gelu_tanh(y)
    z = _dot(y.astype(_BF16), wglu_ref[...])
    yield
    y = _rms(y * _sigmoid(z), g_ref[...])
    for j in range(n_slab):
        ys_ref[j] = y[:, j * LANES:(j + 1) * LANES]
    for b in range(batch):
        for j in range(n_slab):
            lo = b * SSM_WIDTH + j * LANES
            o_ref[:, lo:lo + LANES] = ys_ref[j, pl.ds(b, chunk, stride=batch), :].astype(o_ref.dtype)


def _attn_phases(first_tile, sink_ref, q_ref, kc_ref, kp_ref, vc_ref, vp_ref, g_ref, o_ref, *,
                 n_blocks):
    gq = N_Q_HEADS // N_KV_HEADS
    rows = gq * BLOCK
    qpos = lax.broadcasted_iota(jnp.int32, (rows, BLOCK), 0) % BLOCK
    kidx = lax.broadcasted_iota(jnp.int32, (rows, BLOCK), 1)
    own = kidx <= qpos
    head_of_row = lax.broadcasted_iota(jnp.int32, (rows, 1), 0) // BLOCK
    low_kv = lax.broadcasted_iota(jnp.int32, (2 * BLOCK, KV_WIDTH), 1) < HEAD_DIM
    low_out = lax.broadcasted_iota(jnp.int32, (rows, LANES), 1) < HEAD_DIM
    g = g_ref[...]

    sink_cols = []
    for kv in range(N_KV_HEADS):
        col = jnp.full((rows, 1), sink_ref[0, kv * gq], _F32)
        for j in range(1, gq):
            col = jnp.where(head_of_row == j, sink_ref[0, kv * gq + j], col)
        sink_cols.append(col * LOG2_E)

    for blk in range(n_blocks):
        lo, hi = blk * BLOCK, (blk + 1) * BLOCK
        if blk == 0:
            k_prev, v_prev = kp_ref[...], vp_ref[...]
        else:
            k_prev, v_prev = kc_ref[lo - BLOCK:lo, :], vc_ref[lo - BLOCK:lo, :]
        keys = jnp.concatenate([k_prev, kc_ref[lo:hi, :]], axis=0)
        vals = jnp.concatenate([v_prev, vc_ref[lo:hi, :]], axis=0)
        zero = jnp.zeros_like(keys)
        q_all = jnp.concatenate([q_ref[lo:hi, j * LANES:(j + 1) * LANES] for j in range(gq)], axis=0)
        acc = None
        invs = []
        for kv in range(N_KV_HEADS):
            sel = low_kv if kv == 0 else jnp.logical_not(low_kv)
            k_sel = jnp.where(sel, keys, zero)
            v_sel = jnp.where(sel, vals, zero)
            s = lax.dot_general(q_all, k_sel, (((1,), (1,)), ((), ())),
                                preferred_element_type=_F32)
            s_prev = s[:, :BLOCK]
            if blk == 0:
                s_prev = jnp.where(first_tile, -jnp.inf, s_prev)
            s = jnp.where(own, s[:, BLOCK:], s_prev)
            sink = sink_cols[kv]
            m = jnp.maximum(jnp.max(s, axis=1, keepdims=True), sink)
            e = jnp.exp2(s - m)
            den = jnp.sum(e, axis=1, keepdims=True) + jnp.exp2(sink - m)
            invs.append(1.0 / den)
            p = jnp.concatenate([jnp.where(own, 0.0, e), jnp.where(own, e, 0.0)], axis=1)
            pv = _dot(p.astype(_BF16), v_sel)
            acc = pv if acc is None else acc + pv
        out = acc * jnp.where(low_out, invs[0], invs[1])
        outs = [out[j * BLOCK:(j + 1) * BLOCK, :] for j in range(gq)]
        ms = sum(jnp.sum(o * o, axis=1, keepdims=True) for o in outs) * (1.0 / ATTN_WIDTH)
        scale = lax.rsqrt(ms + RMS_EPS)
        for j in range(gq):
            o_ref[lo:hi, j * LANES:(j + 1) * LANES] = (
                outs[j] * scale * g[:, j * LANES:(j + 1) * LANES]).astype(o_ref.dtype)
        yield


N_S5_IN = 7
N_ATTN_IN = 7
ATTN_AFTER_S5_PHASE = (0, 3, 6, 10)


def _mixer_kernel(*refs, chunk, batch, n_blocks, tiles_per_seq):
    s5_in = refs[:N_S5_IN]
    attn_in = refs[N_S5_IN:N_S5_IN + N_ATTN_IN]
    ssm_ref, attn_ref = refs[N_S5_IN + N_ATTN_IN:N_S5_IN + N_ATTN_IN + 2]
    scratch = refs[N_S5_IN + N_ATTN_IN + 2:]
    step = pl.program_id(0)
    s5 = _s5_phases(step == 0, *s5_in, ssm_ref, *scratch, chunk=chunk, batch=batch)
    attn = _attn_phases(step % tiles_per_seq == 0, *attn_in, attn_ref, n_blocks=n_blocks)
    for phase, _ in enumerate(s5):
        if phase in ATTN_AFTER_S5_PHASE:
            next(attn)
    for _ in attn:
        pass


def _mixer(u_tm, q_tm, k_tm, v_tm, s5w, sinks, g_attn, layer, *, batch, seq):
    chunk = S5_CHUNK
    rows = chunk * batch
    tile = rows
    n_blocks = tile // BLOCK
    tiles_per_seq = seq // tile
    n_slab = SSM_WIDTH // LANES
    const = lambda shape: pl.BlockSpec((None,) + shape, lambda i: (layer,) + (0,) * len(shape))
    cur = lambda w: pl.BlockSpec((tile, w), lambda i: (i % tiles_per_seq, i // tiles_per_seq))
    prev = lambda w: pl.BlockSpec(
        (BLOCK, w),
        lambda i: (jnp.maximum((i % tiles_per_seq) * n_blocks - 1, 0), i // tiles_per_seq))
    return pl.pallas_call(
        functools.partial(_mixer_kernel, chunk=chunk, batch=batch, n_blocks=n_blocks,
                          tiles_per_seq=tiles_per_seq),
        grid=(seq // chunk,),
        in_specs=[
            pl.BlockSpec((chunk, batch * SSM_WIDTH), lambda i: (i, 0)),
            const((N_CLUSTERS, 2, CLUSTER_STATE)),
            const((N_CLUSTERS, CLUSTER_IN, 2 * CLUSTER_STATE)),
            const((N_CLUSTERS, 2 * CLUSTER_STATE, CLUSTER_IN)),
            const((1, SSM_WIDTH)),
            const((SSM_WIDTH, SSM_WIDTH)),
            const((1, SSM_WIDTH)),
            pl.BlockSpec((None, 1, N_Q_HEADS), lambda i: (layer, 0, 0), memory_space=pltpu.SMEM),
            cur(ATTN_WIDTH), cur(KV_WIDTH), prev(KV_WIDTH), cur(KV_WIDTH), prev(KV_WIDTH),
            const((1, ATTN_WIDTH)),
        ],
        out_specs=[pl.BlockSpec((chunk, batch * SSM_WIDTH), lambda i: (i, 0)), cur(ATTN_WIDTH)],
        out_shape=[jax.ShapeDtypeStruct((seq, batch * SSM_WIDTH), _BF16),
                   jax.ShapeDtypeStruct((seq, batch * ATTN_WIDTH), _BF16)],
        scratch_shapes=[
            pltpu.VMEM((n_slab, rows, LANES), _F32),
            pltpu.VMEM((n_slab, rows, LANES), _F32),
            pltpu.VMEM((N_CLUSTERS, rows, 2 * CLUSTER_STATE), _F32),
            pltpu.VMEM((N_CLUSTERS, 2, batch, CLUSTER_STATE), _F32),
        ],
        compiler_params=pltpu.CompilerParams(
            dimension_semantics=("arbitrary",), vmem_limit_bytes=VMEM_LIMIT),
        name="mixer",
    )(u_tm, s5w["lam"], s5w["bbd"], s5w["cbd"], s5w["d_skip"], s5w["w_glu"], s5w["g_ssm"],
      sinks, q_tm, k_tm, k_tm, v_tm, v_tm, g_attn)


def _in_proj(h, g_mix, w_in_ref, u_ref, q_ref, k_ref, v_ref):
    proj = _dot(_rms(h, g_mix).astype(_BF16), w_in_ref[...])
    u_ref[...] = proj[:, :SSM_WIDTH].astype(u_ref.dtype)
    q_ref[...] = (proj[:, SSM_WIDTH:SSM_WIDTH + ATTN_WIDTH] * Q_SCALE).astype(q_ref.dtype)
    k_ref[...] = proj[:, SSM_WIDTH + ATTN_WIDTH:SSM_WIDTH + ATTN_WIDTH + KV_WIDTH].astype(k_ref.dtype)
    v_ref[...] = proj[:, SSM_WIDTH + ATTN_WIDTH + KV_WIDTH:].astype(v_ref.dtype)


def _pre_kernel(h_ref, g_mix_ref, w_in_ref, u_ref, q_ref, k_ref, v_ref):
    _in_proj(h_ref[...], g_mix_ref[...], w_in_ref, u_ref, q_ref, k_ref, v_ref)


def _mid_kernel(*refs, last):
    (h_ref, ssm_ref, attn_ref, p_ref, w_out_ref, g_ffn_ref, w_fi_ref, w_fo_ref, g_ple_ref,
     w_pg_ref, w_pp_ref) = refs[:11]
    if last:
        g_fin_ref, out_ref, acc_ref = refs[11:]
    else:
        g_mix_ref, w_in_ref, h_out_ref, u_ref, q_ref, k_ref, v_ref, acc_ref = refs[11:]

    h = (h_ref[...] + _dot(ssm_ref[...], w_out_ref[:SSM_WIDTH, :])
         + _dot(attn_ref[...], w_out_ref[SSM_WIDTH:, :]))
    hn = _rms(h, g_ffn_ref[...]).astype(_BF16)
    for c in range(FFN_HIDDEN // FFN_CHUNK):
        lo = c * FFN_CHUNK
        gate = _dot(hn, w_fi_ref[:, lo:lo + FFN_CHUNK])
        up = _dot(hn, w_fi_ref[:, FFN_HIDDEN + lo:FFN_HIDDEN + lo + FFN_CHUNK])
        act = (gate * _sigmoid(gate) * up).astype(_BF16)
        part = _dot(act, w_fo_ref[lo:lo + FFN_CHUNK, :])
        if c == 0:
            acc_ref[...] = part
        else:
            acc_ref[...] += part
    h = h + acc_ref[...]
    gate = _sigmoid(_dot(_rms(h, g_ple_ref[...]).astype(_BF16), w_pg_ref[...]))
    h = h + gate * _dot(p_ref[...].astype(_BF16), w_pp_ref[...])
    if last:
        out_ref[...] = _rms(h, g_fin_ref[...])
    else:
        h_out_ref[...] = h
        _in_proj(h, g_mix_ref[...], w_in_ref, u_ref, q_ref, k_ref, v_ref)


def _layer_const(layer, shape):
    return pl.BlockSpec((None,) + shape, lambda b, i: (layer,) + (0,) * len(shape),
                        pipeline_mode=pl.Buffered(1))


def _proj_out_specs(tile, batch, seq):
    tm = lambda w: pl.BlockSpec((tile, w), lambda b, i: (i, b))
    specs = [tm(SSM_WIDTH), tm(ATTN_WIDTH), tm(KV_WIDTH), tm(KV_WIDTH)]
    shapes = [jax.ShapeDtypeStruct((seq, batch * w), _BF16)
              for w in (SSM_WIDTH, ATTN_WIDTH, KV_WIDTH, KV_WIDTH)]
    return specs, shapes


def _pre(x2d, g_mix, w_in, *, batch, seq):
    tile = TOKEN_TILE
    n_t = seq // tile
    specs, shapes = _proj_out_specs(tile, batch, seq)
    return pl.pallas_call(
        _pre_kernel,
        grid=(batch, n_t),
        in_specs=[
            pl.BlockSpec((tile, D_MODEL), lambda b, i: (b * n_t + i, 0)),
            _layer_const(0, (1, D_MODEL)),
            _layer_const(0, (D_MODEL, IN_WIDTH)),
        ],
        out_specs=specs,
        out_shape=shapes,
        compiler_params=pltpu.CompilerParams(
            dimension_semantics=("parallel", "parallel"), vmem_limit_bytes=VMEM_LIMIT),
        name="pre_in_proj",
    )(x2d, g_mix, w_in)


def _mid(h2d, ssm_tm, attn_tm, p2d, weights, layer, *, batch, seq, depth, h_batch_major):
    tile = TOKEN_TILE
    n_t = seq // tile
    last = layer == depth - 1
    bm = lambda w, off=0: pl.BlockSpec((tile, w), lambda b, i: (off + b * n_t + i, 0))
    tm = lambda w: pl.BlockSpec((tile, w), lambda b, i: (i, b))
    in_specs = [
        bm(D_MODEL) if h_batch_major else tm(D_MODEL),
        tm(SSM_WIDTH), tm(ATTN_WIDTH),
        bm(PLE_DIM, layer * batch * n_t),
        _layer_const(layer, (D_MODEL, D_MODEL)),
        _layer_const(layer, (1, D_MODEL)),
        _layer_const(layer, (D_MODEL, 2 * FFN_HIDDEN)),
        _layer_const(layer, (FFN_HIDDEN, D_MODEL)),
        _layer_const(layer, (1, D_MODEL)),
        _layer_const(layer, (D_MODEL, D_MODEL)),
        _layer_const(layer, (PLE_DIM, D_MODEL)),
    ]
    args = [h2d, ssm_tm, attn_tm, p2d, weights["w_out"], weights["g_ffn"], weights["w_fi"],
            weights["w_fo"], weights["g_ple"], weights["w_pg"], weights["w_pp"]]
    if last:
        in_specs.append(pl.BlockSpec((1, D_MODEL), lambda b, i: (0, 0)))
        args.append(weights["g_fin"])
        out_specs = [bm(D_MODEL)]
        out_shape = [jax.ShapeDtypeStruct((batch * seq, D_MODEL), _F32)]
    else:
        in_specs += [_layer_const(layer + 1, (1, D_MODEL)),
                     _layer_const(layer + 1, (D_MODEL, IN_WIDTH))]
        args += [weights["g_mix"], weights["w_in"]]
        specs, shapes = _proj_out_specs(tile, batch, seq)
        out_specs = [tm(D_MODEL)] + specs
        out_shape = [jax.ShapeDtypeStruct((seq, batch * D_MODEL), _F32)] + shapes
    return pl.pallas_call(
        functools.partial(_mid_kernel, last=last),
        grid=(batch, n_t),
        in_specs=in_specs,
        out_specs=out_specs,
        out_shape=out_shape,
        scratch_shapes=[pltpu.VMEM((tile, D_MODEL), _F32)],
        compiler_params=pltpu.CompilerParams(
            dimension_semantics=("parallel", "parallel"), vmem_limit_bytes=VMEM_LIMIT),
        name="mid_last" if last else "mid",
    )(*args)


def _pair_heads(a, axis):
    gq = N_Q_HEADS // N_KV_HEADS
    shape = a.shape
    a = a.reshape(shape[:axis] + (N_KV_HEADS, gq, HEAD_DIM) + shape[axis + 1:])
    return jnp.swapaxes(a, axis, axis + 1).reshape(shape)


def kernel(x, p, norm_mix, w_in, ssm_a_re, ssm_a_im, ssm_log_dt, ssm_b_re, ssm_b_im, ssm_c_re,
           ssm_c_im, ssm_d, ssm_w_glu, attn_sinks, norm_ssm_out, norm_attn_out, w_out, norm_ffn,
           w_ffn_in, w_ffn_out, norm_ple, w_ple_gate, w_ple_proj, norm_final):
    batch, seq, _ = x.shape
    depth = w_in.shape[0]
    assert batch == SUBLANES and seq % max(TOKEN_TILE, S5_CHUNK * batch) == 0

    row = lambda a: a.reshape(a.shape[0], 1, a.shape[-1])
    q_lo, q_hi = SSM_WIDTH, SSM_WIDTH + ATTN_WIDTH
    weights = {
        "g_mix": row(norm_mix),
        "w_in": jnp.concatenate(
            [w_in[:, :, :q_lo], _pair_heads(w_in[:, :, q_lo:q_hi], 2), w_in[:, :, q_hi:]],
            axis=2).astype(_BF16),
        "w_out": jnp.concatenate(
            [w_out[:, :SSM_WIDTH, :], _pair_heads(w_out[:, SSM_WIDTH:, :], 1)],
            axis=1).astype(_BF16),
        "g_ffn": row(norm_ffn),
        "w_fi": w_ffn_in.astype(_BF16),
        "w_fo": w_ffn_out.astype(_BF16),
        "g_ple": row(norm_ple),
        "w_pg": w_ple_gate.astype(_BF16),
        "w_pp": w_ple_proj.astype(_BF16),
        "g_fin": norm_final.reshape(1, D_MODEL),
    }
    g_attn = row(_pair_heads(norm_attn_out, 1))
    sinks = row(attn_sinks)

    lam, bbd, cbd = _s5_prep(ssm_a_re, ssm_a_im, ssm_log_dt, ssm_b_re, ssm_b_im, ssm_c_re, ssm_c_im)
    s5w = {"lam": lam, "bbd": bbd, "cbd": cbd, "d_skip": row(ssm_d),
           "w_glu": ssm_w_glu.astype(_BF16), "g_ssm": row(norm_ssm_out)}

    p2d = p.reshape(depth * batch * seq, PLE_DIM)
    h2d = x.reshape(batch * seq, D_MODEL)
    u, q, k, v = _pre(h2d, weights["g_mix"], weights["w_in"], batch=batch, seq=seq)
    for layer in range(depth):
        ssm, attn = _mixer(u, q, k, v, s5w, sinks, g_attn, layer, batch=batch, seq=seq)
        outs = _mid(h2d, ssm, attn, p2d, weights, layer,
                    batch=batch, seq=seq, depth=depth, h_batch_major=(layer == 0))
        if layer == depth - 1:
            return outs[0].reshape(batch, seq, D_MODEL)
        h2d, u, q, k, v = outs
```

```python
import functools
import math

import jax
import jax.numpy as jnp
from jax import lax
from jax.experimental import pallas as pl
from jax.experimental.pallas import tpu as pltpu

D_MODEL = 1024
SSM_WIDTH = 512
SSM_GROUP_WIDTH = 16
N_SSM_GROUPS = 32
SSM_STATE = 64
HEAD_DIM = 64
N_Q_HEADS = 8
N_KV_HEADS = 2
ATTN_WIDTH = 512
KV_WIDTH = 128
IN_WIDTH = 1280
WINDOW = 128
BLOCK = 128
FFN_HIDDEN = 2816
PLE_DIM = 256
RMS_EPS = 1e-6

SUBLANES = 8
LANES = 128
MXU_DIM = 256

CLUSTER_GROUPS = MXU_DIM // SSM_GROUP_WIDTH
N_CLUSTERS = N_SSM_GROUPS // CLUSTER_GROUPS
CLUSTER_IN = CLUSTER_GROUPS * SSM_GROUP_WIDTH
CLUSTER_STATE = CLUSTER_GROUPS * SSM_STATE

TOKEN_TILE = 1024
PRE_TILE = 1024
S5_CHUNK = 64
FFN_CHUNK = 256
VMEM_LIMIT = 60 * 1024 * 1024
LOG2_E = math.log2(math.e)
Q_SCALE = HEAD_DIM ** -0.5 * LOG2_E

_F32 = jnp.float32
_BF16 = jnp.bfloat16


def _dot(a, b):
    return jnp.dot(a, b, preferred_element_type=_F32)


def _rms(x, g):
    ms = jnp.mean(x * x, axis=-1, keepdims=True)
    return x * lax.rsqrt(ms + RMS_EPS) * g


def _sigmoid(x):
    return 1.0 / (1.0 + jnp.exp2(x * (-LOG2_E)))


def _gelu_tanh(x):
    k = -2.0 * LOG2_E * math.sqrt(2.0 / math.pi)
    t = x * (x * x * (k * 0.044715) + k)
    return x / (1.0 + jnp.exp2(t))


def _s5_prep_kernel(ar_ref, ai_ref, ldt_ref, btr_ref, bti_ref, cr_ref, ci_ref,
                    lbr_ref, lbi_ref, bbd_ref, cbd_ref):
    G, H, P = N_SSM_GROUPS, SSM_GROUP_WIDTH, SSM_STATE
    ar = ar_ref[...]
    ai = ai_ref[...]
    dt = jnp.exp(ldt_ref[...])
    mag = jnp.exp(ar * dt)
    lb_re = mag * jnp.cos(ai * dt)
    lb_im = mag * jnp.sin(ai * dt)
    den = ar * ar + ai * ai
    nr = lb_re - 1.0
    ni = lb_im
    f_re = (nr * ar + ni * ai) / den
    f_im = (ni * ar - nr * ai) / den
    lbr_ref[...] = lb_re
    lbi_ref[...] = lb_im
    cr = cr_ref[...]
    ci = ci_ref[...]
    fr = f_re[:, None, :]
    fi = f_im[:, None, :]
    cf_re = (cr * fr - ci * fi).reshape(G * H, P)
    cf_im_neg = (-(cr * fi + ci * fr)).reshape(G * H, P)

    spread = (lax.broadcasted_iota(jnp.int32, (P, CLUSTER_STATE), 1) % P
              == lax.broadcasted_iota(jnp.int32, (P, CLUSTER_STATE), 0)).astype(_BF16)
    own_group = (lax.broadcasted_iota(jnp.int32, (CLUSTER_IN, CLUSTER_STATE), 0) // H
                 == lax.broadcasted_iota(jnp.int32, (CLUSTER_IN, CLUSTER_STATE), 1) // P)

    def block_diag(rows_gh_p):
        return jnp.where(own_group, _dot(rows_gh_p.astype(_BF16), spread), 0.0)

    for c in range(N_CLUSTERS):
        rows = slice(c * CLUSTER_IN, (c + 1) * CLUSTER_IN)
        for part, (bt_ref, cf) in enumerate(((btr_ref, cf_re), (bti_ref, cf_im_neg))):
            cols = slice(part * CLUSTER_STATE, (part + 1) * CLUSTER_STATE)
            bbd_ref[c, :, cols] = block_diag(bt_ref[rows, :]).astype(bbd_ref.dtype)
            cbd_ref[c, cols, :] = block_diag(cf[rows, :]).T.astype(cbd_ref.dtype)


def _s5_prep(a_re, a_im, log_dt, b_re, b_im, c_re, c_im):
    depth = a_re.shape[0]
    G, P, H, C = N_SSM_GROUPS, SSM_STATE, SSM_GROUP_WIDTH, N_CLUSTERS
    gp = pl.BlockSpec((None, G, P), lambda i: (i, 0, 0))
    g1 = pl.BlockSpec((None, G, 1), lambda i: (i, 0, 0))
    gh_p = pl.BlockSpec((None, G * H, P), lambda i: (i, 0, 0))
    ghp = pl.BlockSpec((None, G, H, P), lambda i: (i, 0, 0, 0))
    rows_gh = lambda b: jnp.swapaxes(b, 2, 3).reshape(depth, G * H, P)
    lb_re, lb_im, bbd, cbd = pl.pallas_call(
        _s5_prep_kernel,
        grid=(depth,),
        in_specs=[gp, gp, g1, gh_p, gh_p, ghp, ghp],
        out_specs=[gp, gp,
                   pl.BlockSpec((None, C, CLUSTER_IN, 2 * CLUSTER_STATE), lambda i: (i, 0, 0, 0)),
                   pl.BlockSpec((None, C, 2 * CLUSTER_STATE, CLUSTER_IN), lambda i: (i, 0, 0, 0))],
        out_shape=[jax.ShapeDtypeStruct((depth, G, P), _F32)] * 2
        + [jax.ShapeDtypeStruct((depth, C, CLUSTER_IN, 2 * CLUSTER_STATE), _BF16),
           jax.ShapeDtypeStruct((depth, C, 2 * CLUSTER_STATE, CLUSTER_IN), _BF16)],
        name="s5_prep",
    )(a_re, a_im, log_dt.reshape(depth, G, 1), rows_gh(b_re), rows_gh(b_im), c_re, c_im)
    lam = jnp.stack([lb_re.reshape(depth, C, CLUSTER_STATE), lb_im.reshape(depth, C, CLUSTER_STATE)],
                    axis=2)
    return lam, bbd, cbd


SCAN_SLICES = 4


def _s5_phases(first_step, u_ref, lam_ref, bbd_ref, cbd_ref, d_ref, wglu_ref, g_ref, o_ref,
               us_ref, ys_ref, bu_ref, st_ref, *, chunk, batch):
    n_slab = SSM_WIDTH // LANES
    slabs_per_cluster = CLUSTER_IN // LANES

    @pl.when(first_step)
    def _():
        st_ref[...] = jnp.zeros_like(st_ref)

    for b in range(batch):
        for j in range(n_slab):
            lo = b * SSM_WIDTH + j * LANES
            us_ref[j, pl.ds(b, chunk, stride=batch), :] = u_ref[:, lo:lo + LANES].astype(_F32)
    for c in range(N_CLUSTERS):
        u_c = jnp.concatenate(
            [us_ref[c * slabs_per_cluster + j] for j in range(slabs_per_cluster)], axis=1)
        bu_ref[c] = _dot(u_c.astype(_BF16), bbd_ref[c])
    yield

    ys = []
    for c in range(N_CLUSTERS):
        lr = jnp.broadcast_to(lam_ref[c, 0:1, :], (batch, CLUSTER_STATE))
        li = jnp.broadcast_to(lam_ref[c, 1:2, :], (batch, CLUSTER_STATE))
        sr, si = st_ref[c, 0], st_ref[c, 1]
        for t in range(chunk):
            r0 = t * batch
            br = bu_ref[c, r0:r0 + batch, 0:CLUSTER_STATE]
            bi = bu_ref[c, r0:r0 + batch, CLUSTER_STATE:2 * CLUSTER_STATE]
            sr, si = lr * sr - li * si + br, lr * si + li * sr + bi
            bu_ref[c, r0:r0 + batch, 0:CLUSTER_STATE] = sr
            bu_ref[c, r0:r0 + batch, CLUSTER_STATE:2 * CLUSTER_STATE] = si
            if (t + 1) % (chunk // SCAN_SLICES) == 0:
                yield
        st_ref[c, 0] = sr
        st_ref[c, 1] = si
        ys.append(_dot(bu_ref[c].astype(_BF16), cbd_ref[c]))
        yield
    u_all = jnp.concatenate([us_ref[j] for j in range(n_slab)], axis=1)
    y = jnp.concatenate(ys, axis=1) + d_ref[...] * u_all
    y = _gelu_tanh(y)
    z = _dot(y.astype(_BF16), wglu_ref[...])
    yield
    y = _rms(y * _sigmoid(z), g_ref[...])
    for j in range(n_slab):
        ys_ref[j] = y[:, j * LANES:(j + 1) * LANES]
    for b in range(batch):
        for j in range(n_slab):
            lo = b * SSM_WIDTH + j * LANES
            o_ref[:, lo:lo + LANES] = ys_ref[j, pl.ds(b, chunk, stride=batch), :].astype(o_ref.dtype)


def _attn_phases(first_tile, sink_ref, q_ref, kc_ref, kp_ref, vc_ref, vp_ref, g_ref, o_ref, *,
                 n_blocks):
    gq = N_Q_HEADS // N_KV_HEADS
    rows = gq * BLOCK
    qpos = lax.broadcasted_iota(jnp.int32, (rows, BLOCK), 0) % BLOCK
    kidx = lax.broadcasted_iota(jnp.int32, (rows, BLOCK), 1)
    own = kidx <= qpos
    head_of_row = lax.broadcasted_iota(jnp.int32, (rows, 1), 0) // BLOCK
    low_kv = lax.broadcasted_iota(jnp.int32, (2 * BLOCK, KV_WIDTH), 1) < HEAD_DIM
    low_out = lax.broadcasted_iota(jnp.int32, (rows, LANES), 1) < HEAD_DIM
    g = g_ref[...]

    sink_cols = []
    for kv in range(N_KV_HEADS):
        col = jnp.full((rows, 1), sink_ref[0, kv * gq], _F32)
        for j in range(1, gq):
            col = jnp.where(head_of_row == j, sink_ref[0, kv * gq + j], col)
        sink_cols.append(col * LOG2_E)

    for blk in range(n_blocks):
        lo, hi = blk * BLOCK, (blk + 1) * BLOCK
        if blk == 0:
            k_prev, v_prev = kp_ref[...], vp_ref[...]
        else:
            k_prev, v_prev = kc_ref[lo - BLOCK:lo, :], vc_ref[lo - BLOCK:lo, :]
        keys = jnp.concatenate([k_prev, kc_ref[lo:hi, :]], axis=0)
        vals = jnp.concatenate([v_prev, vc_ref[lo:hi, :]], axis=0)
        zero = jnp.zeros_like(keys)
        q_all = jnp.concatenate([q_ref[lo:hi, j * LANES:(j + 1) * LANES] for j in range(gq)], axis=0)
        acc = None
        invs = []
        for kv in range(N_KV_HEADS):
            sel = low_kv if kv == 0 else jnp.logical_not(low_kv)
            k_sel = jnp.where(sel, keys, zero)
            v_sel = jnp.where(sel, vals, zero)
            s = lax.dot_general(q_all, k_sel, (((1,), (1,)), ((), ())),
                                preferred_element_type=_F32)
            s_prev = s[:, :BLOCK]
            if blk == 0:
                s_prev = jnp.where(first_tile, -jnp.inf, s_prev)
            s = jnp.where(own, s[:, BLOCK:], s_prev)
            sink = sink_cols[kv]
            m = jnp.maximum(jnp.max(s, axis=1, keepdims=True), sink)
            e = jnp.exp2(s - m)
            den = jnp.sum(e, axis=1, keepdims=True) + jnp.exp2(sink - m)
            invs.append(1.0 / den)
            p = jnp.concatenate([jnp.where(own, 0.0, e), jnp.where(own, e, 0.0)], axis=1)
            pv = _dot(p.astype(_BF16), v_sel)
            acc = pv if acc is None else acc + pv
        out = acc * jnp.where(low_out, invs[0], invs[1])
        outs = [out[j * BLOCK:(j + 1) * BLOCK, :] for j in range(gq)]
        ms = sum(jnp.sum(o * o, axis=1, keepdims=True) for o in outs) * (1.0 / ATTN_WIDTH)
        scale = lax.rsqrt(ms + RMS_EPS)
        for j in range(gq):
            o_ref[lo:hi, j * LANES:(j + 1) * LANES] = (
                outs[j] * scale * g[:, j * LANES:(j + 1) * LANES]).astype(o_ref.dtype)
        yield


N_S5_IN = 7
N_ATTN_IN = 7
ATTN_AFTER_S5_PHASE = (0, 3, 6, 10)


def _mixer_kernel(*refs, chunk, batch, n_blocks, tiles_per_seq):
    s5_in = refs[:N_S5_IN]
    attn_in = refs[N_S5_IN:N_S5_IN + N_ATTN_IN]
    ssm_ref, attn_ref = refs[N_S5_IN + N_ATTN_IN:N_S5_IN + N_ATTN_IN + 2]
    scratch = refs[N_S5_IN + N_ATTN_IN + 2:]
    step = pl.program_id(0)
    s5 = _s5_phases(step == 0, *s5_in, ssm_ref, *scratch, chunk=chunk, batch=batch)
    attn = _attn_phases(step % tiles_per_seq == 0, *attn_in, attn_ref, n_blocks=n_blocks)
    for phase, _ in enumerate(s5):
        if phase in ATTN_AFTER_S5_PHASE:
            next(attn)
    for _ in attn:
        pass


def _mixer(u_tm, q_tm, k_tm, v_tm, s5w, sinks, g_attn, layer, *, batch, seq):
    chunk = S5_CHUNK
    rows = chunk * batch
    tile = rows
    n_blocks = tile // BLOCK
    tiles_per_seq = seq // tile
    n_slab = SSM_WIDTH // LANES
    const = lambda shape: pl.BlockSpec((None,) + shape, lambda i: (layer,) + (0,) * len(shape))
    cur = lambda w: pl.BlockSpec((tile, w), lambda i: (i % tiles_per_seq, i // tiles_per_seq))
    prev = lambda w: pl.BlockSpec(
        (BLOCK, w),
        lambda i: (jnp.maximum((i % tiles_per_seq) * n_blocks - 1, 0), i // tiles_per_seq))
    return pl.pallas_call(
        functools.partial(_mixer_kernel, chunk=chunk, batch=batch, n_blocks=n_blocks,
                          tiles_per_seq=tiles_per_seq),
        grid=(seq // chunk,),
        in_specs=[
            pl.BlockSpec((chunk, batch * SSM_WIDTH), lambda i: (i, 0)),
            const((N_CLUSTERS, 2, CLUSTER_STATE)),
            const((N_CLUSTERS, CLUSTER_IN, 2 * CLUSTER_STATE)),
            const((N_CLUSTERS, 2 * CLUSTER_STATE, CLUSTER_IN)),
            const((1, SSM_WIDTH)),
            const((SSM_WIDTH, SSM_WIDTH)),
            const((1, SSM_WIDTH)),
            pl.BlockSpec((None, 1, N_Q_HEADS), lambda i: (layer, 0, 0), memory_space=pltpu.SMEM),
            cur(ATTN_WIDTH), cur(KV_WIDTH), prev(KV_WIDTH), cur(KV_WIDTH), prev(KV_WIDTH),
            const((1, ATTN_WIDTH)),
        ],
        out_specs=[pl.BlockSpec((chunk, batch * SSM_WIDTH), lambda i: (i, 0)), cur(ATTN_WIDTH)],
        out_shape=[jax.ShapeDtypeStruct((seq, batch * SSM_WIDTH), _BF16),
                   jax.ShapeDtypeStruct((seq, batch * ATTN_WIDTH), _BF16)],
        scratch_shapes=[
            pltpu.VMEM((n_slab, rows, LANES), _F32),
            pltpu.VMEM((n_slab, rows, LANES), _F32),
            pltpu.VMEM((N_CLUSTERS, rows, 2 * CLUSTER_STATE), _F32),
            pltpu.VMEM((N_CLUSTERS, 2, batch, CLUSTER_STATE), _F32),
        ],
        compiler_params=pltpu.CompilerParams(
            dimension_semantics=("arbitrary",), vmem_limit_bytes=VMEM_LIMIT),
        name="mixer",
    )(u_tm, s5w["lam"], s5w["bbd"], s5w["cbd"], s5w["d_skip"], s5w["w_glu"], s5w["g_ssm"],
      sinks, q_tm, k_tm, k_tm, v_tm, v_tm, g_attn)


def _in_proj(h, g_mix, w_in_ref, u_ref, q_ref, k_ref, v_ref):
    proj = _dot(_rms(h, g_mix).astype(_BF16), w_in_ref[...])
    u_ref[...] = proj[:, :SSM_WIDTH].astype(u_ref.dtype)
    q_ref[...] = (proj[:, SSM_WIDTH:SSM_WIDTH + ATTN_WIDTH] * Q_SCALE).astype(q_ref.dtype)
    k_ref[...] = proj[:, SSM_WIDTH + ATTN_WIDTH:SSM_WIDTH + ATTN_WIDTH + KV_WIDTH].astype(k_ref.dtype)
    v_ref[...] = proj[:, SSM_WIDTH + ATTN_WIDTH + KV_WIDTH:].astype(v_ref.dtype)


def _pre_kernel(h_ref, g_mix_ref, w_in_ref, u_ref, q_ref, k_ref, v_ref):
    _in_proj(h_ref[...], g_mix_ref[...], w_in_ref, u_ref, q_ref, k_ref, v_ref)


def _mid_kernel(*refs, last):
    (h_ref, ssm_ref, attn_ref, p_ref, w_out_ref, g_ffn_ref, w_fi_ref, w_fo_ref, g_ple_ref,
     w_pg_ref, w_pp_ref) = refs[:11]
    if last:
        g_fin_ref, out_ref = refs[11:]
    else:
        g_mix_ref, w_in_ref, h_out_ref, u_ref, q_ref, k_ref, v_ref = refs[11:]

    res_ref = out_ref if last else h_out_ref
    res_ref[...] = (h_ref[...] + _dot(ssm_ref[...], w_out_ref[:SSM_WIDTH, :])
                    + _dot(attn_ref[...], w_out_ref[SSM_WIDTH:, :]))
    hn = _rms(res_ref[...], g_ffn_ref[...]).astype(_BF16)
    for c in range(FFN_HIDDEN // FFN_CHUNK):
        lo = c * FFN_CHUNK
        gate = _dot(hn, w_fi_ref[:, lo:lo + FFN_CHUNK])
        up = _dot(hn, w_fi_ref[:, FFN_HIDDEN + lo:FFN_HIDDEN + lo + FFN_CHUNK])
        act = (gate * _sigmoid(gate) * up).astype(_BF16)
        res_ref[...] += _dot(act, w_fo_ref[lo:lo + FFN_CHUNK, :])
    h = res_ref[...]
    gate = _sigmoid(_dot(_rms(h, g_ple_ref[...]).astype(_BF16), w_pg_ref[...]))
    h = h + gate * _dot(p_ref[...].astype(_BF16), w_pp_ref[...])
    if last:
        out_ref[...] = _rms(h, g_fin_ref[...])
    else:
        h_out_ref[...] = h
        _in_proj(h, g_mix_ref[...], w_in_ref, u_ref, q_ref, k_ref, v_ref)


def _layer_const(layer, shape):
    return pl.BlockSpec((None,) + shape, lambda b, i: (layer,) + (0,) * len(shape),
                        pipeline_mode=pl.Buffered(1))


def _proj_out_specs(tile, batch, seq):
    tm = lambda w: pl.BlockSpec((tile, w), lambda b, i: (i, b))
    specs = [tm(SSM_WIDTH), tm(ATTN_WIDTH), tm(KV_WIDTH), tm(KV_WIDTH)]
    shapes = [jax.ShapeDtypeStruct((seq, batch * w), _BF16)
              for w in (SSM_WIDTH, ATTN_WIDTH, KV_WIDTH, KV_WIDTH)]
    return specs, shapes


def _pre(x2d, g_mix, w_in, *, batch, seq):
    tile = PRE_TILE
    n_t = seq // tile
    specs, shapes = _proj_out_specs(tile, batch, seq)
    return pl.pallas_call(
        _pre_kernel,
        grid=(batch, n_t),
        in_specs=[
            pl.BlockSpec((tile, D_MODEL), lambda b, i: (b * n_t + i, 0)),
            _layer_const(0, (1, D_MODEL)),
            _layer_const(0, (D_MODEL, IN_WIDTH)),
        ],
        out_specs=specs,
        out_shape=shapes,
        compiler_params=pltpu.CompilerParams(
            dimension_semantics=("parallel", "parallel"), vmem_limit_bytes=VMEM_LIMIT),
        name="pre_in_proj",
    )(x2d, g_mix, w_in)


def _mid(h2d, ssm_tm, attn_tm, p2d, weights, layer, *, batch, seq, depth, h_batch_major):
    tile = TOKEN_TILE
    n_t = seq // tile
    last = layer == depth - 1
    bm = lambda w, off=0: pl.BlockSpec((tile, w), lambda b, i: (off + b * n_t + i, 0))
    tm = lambda w: pl.BlockSpec((tile, w), lambda b, i: (i, b))
    in_specs = [
        bm(D_MODEL) if h_batch_major else tm(D_MODEL),
        tm(SSM_WIDTH), tm(ATTN_WIDTH),
        bm(PLE_DIM, layer * batch * n_t),
        _layer_const(layer, (D_MODEL, D_MODEL)),
        _layer_const(layer, (1, D_MODEL)),
        _layer_const(layer, (D_MODEL, 2 * FFN_HIDDEN)),
        _layer_const(layer, (FFN_HIDDEN, D_MODEL)),
        _layer_const(layer, (1, D_MODEL)),
        _layer_const(layer, (D_MODEL, D_MODEL)),
        _layer_const(layer, (PLE_DIM, D_MODEL)),
    ]
    args = [h2d, ssm_tm, attn_tm, p2d, weights["w_out"], weights["g_ffn"], weights["w_fi"],
            weights["w_fo"], weights["g_ple"], weights["w_pg"], weights["w_pp"]]
    if last:
        in_specs.append(pl.BlockSpec((1, D_MODEL), lambda b, i: (0, 0)))
        args.append(weights["g_fin"])
        out_specs = [bm(D_MODEL)]
        out_shape = [jax.ShapeDtypeStruct((batch * seq, D_MODEL), _F32)]
    else:
        in_specs += [_layer_const(layer + 1, (1, D_MODEL)),
                     _layer_const(layer + 1, (D_MODEL, IN_WIDTH))]
        args += [weights["g_mix"], weights["w_in"]]
        specs, shapes = _proj_out_specs(tile, batch, seq)
        out_specs = [tm(D_MODEL)] + specs
        out_shape = [jax.ShapeDtypeStruct((seq, batch * D_MODEL), _F32)] + shapes
    return pl.pallas_call(
        functools.partial(_mid_kernel, last=last),
        grid=(batch, n_t),
        in_specs=in_specs,
        out_specs=out_specs,
        out_shape=out_shape,
        compiler_params=pltpu.CompilerParams(
            dimension_semantics=("parallel", "parallel"), vmem_limit_bytes=VMEM_LIMIT),
        name="mid_last" if last else "mid",
    )(*args)


def _pair_heads(a, axis):
    gq = N_Q_HEADS // N_KV_HEADS
    shape = a.shape
    a = a.reshape(shape[:axis] + (N_KV_HEADS, gq, HEAD_DIM) + shape[axis + 1:])
    return jnp.swapaxes(a, axis, axis + 1).reshape(shape)


def kernel(x, p, norm_mix, w_in, ssm_a_re, ssm_a_im, ssm_log_dt, ssm_b_re, ssm_b_im, ssm_c_re,
           ssm_c_im, ssm_d, ssm_w_glu, attn_sinks, norm_ssm_out, norm_attn_out, w_out, norm_ffn,
           w_ffn_in, w_ffn_out, norm_ple, w_ple_gate, w_ple_proj, norm_final):
    batch, seq, _ = x.shape
    depth = w_in.shape[0]
    assert batch == SUBLANES and seq % max(TOKEN_TILE, PRE_TILE, S5_CHUNK * batch) == 0

    row = lambda a: a.reshape(a.shape[0], 1, a.shape[-1])
    q_lo, q_hi = SSM_WIDTH, SSM_WIDTH + ATTN_WIDTH
    weights = {
        "g_mix": row(norm_mix),
        "w_in": jnp.concatenate(
            [w_in[:, :, :q_lo], _pair_heads(w_in[:, :, q_lo:q_hi], 2), w_in[:, :, q_hi:]],
            axis=2).astype(_BF16),
        "w_out": jnp.concatenate(
            [w_out[:, :SSM_WIDTH, :], _pair_heads(w_out[:, SSM_WIDTH:, :], 1)],
            axis=1).astype(_BF16),
        "g_ffn": row(norm_ffn),
        "w_fi": w_ffn_in.astype(_BF16),
        "w_fo": w_ffn_out.astype(_BF16),
        "g_ple": row(norm_ple),
        "w_pg": w_ple_gate.astype(_BF16),
        "w_pp": w_ple_proj.astype(_BF16),
        "g_fin": norm_final.reshape(1, D_MODEL),
    }
    g_attn = row(_pair_heads(norm_attn_out, 1))
    sinks = row(attn_sinks)

    lam, bbd, cbd = _s5_prep(ssm_a_re, ssm_a_im, ssm_log_dt, ssm_b_re, ssm_b_im, ssm_c_re, ssm_c_im)
    s5w = {"lam": lam, "bbd": bbd, "cbd": cbd, "d_skip": row(ssm_d),
           "w_glu": ssm_w_glu.astype(_BF16), "g_ssm": row(norm_ssm_out)}

    p2d = p.reshape(depth * batch * seq, PLE_DIM)
    h2d = x.reshape(batch * seq, D_MODEL)
    u, q, k, v = _pre(h2d, weights["g_mix"], weights["w_in"], batch=batch, seq=seq)
    for layer in range(depth):
        ssm, attn = _mixer(u, q, k, v, s5w, sinks, g_attn, layer, batch=batch, seq=seq)
        outs = _mid(h2d, ssm, attn, p2d, weights, layer,
                    batch=batch, seq=seq, depth=depth, h_batch_major=(layer == 0))
        if layer == depth - 1:
            return outs[0].reshape(batch, seq, D_MODEL)
        h2d, u, q, k, v = outs
```

```python
import functools
import math

import jax
import jax.numpy as jnp
from jax import lax
from jax.experimental import pallas as pl
from jax.experimental.pallas import tpu as pltpu

D_MODEL = 1024
SSM_WIDTH = 512
SSM_GROUP_WIDTH = 16
N_SSM_GROUPS = 32
SSM_STATE = 64
HEAD_DIM = 64
N_Q_HEADS = 8
N_KV_HEADS = 2
ATTN_WIDTH = 512
KV_WIDTH = 128
IN_WIDTH = 1280
WINDOW = 128
BLOCK = 128
FFN_HIDDEN = 2816
PLE_DIM = 256
RMS_EPS = 1e-6

SUBLANES = 8
LANES = 128
MXU_DIM = 256

CLUSTER_GROUPS = MXU_DIM // SSM_GROUP_WIDTH
N_CLUSTERS = N_SSM_GROUPS // CLUSTER_GROUPS
CLUSTER_IN = CLUSTER_GROUPS * SSM_GROUP_WIDTH
CLUSTER_STATE = CLUSTER_GROUPS * SSM_STATE

TOKEN_TILE = 1024
PRE_TILE = 1024
S5_CHUNK = 64
FFN_CHUNK = 256
VMEM_LIMIT = 60 * 1024 * 1024
LOG2_E = math.log2(math.e)
Q_SCALE = HEAD_DIM ** -0.5 * LOG2_E

_F32 = jnp.float32
_BF16 = jnp.bfloat16


def _dot(a, b):
    return jnp.dot(a, b, preferred_element_type=_F32)


def _rms(x, g):
    ms = jnp.mean(x * x, axis=-1, keepdims=True)
    return x * lax.rsqrt(ms + RMS_EPS) * g


def _sigmoid(x):
    return 1.0 / (1.0 + jnp.exp2(x * (-LOG2_E)))


def _gelu_tanh(x):
    k = -2.0 * LOG2_E * math.sqrt(2.0 / math.pi)
    t = x * (x * x * (k * 0.044715) + k)
    return x / (1.0 + jnp.exp2(t))


def _s5_prep_kernel(ar_ref, ai_ref, ldt_ref, btr_ref, bti_ref, cr_ref, ci_ref,
                    lbr_ref, lbi_ref, bbd_ref, cbd_ref):
    G, H, P = N_SSM_GROUPS, SSM_GROUP_WIDTH, SSM_STATE
    ar = ar_ref[...]
    ai = ai_ref[...]
    dt = jnp.exp(ldt_ref[...])
    mag = jnp.exp(ar * dt)
    lb_re = mag * jnp.cos(ai * dt)
    lb_im = mag * jnp.sin(ai * dt)
    den = ar * ar + ai * ai
    nr = lb_re - 1.0
    ni = lb_im
    f_re = (nr * ar + ni * ai) / den
    f_im = (ni * ar - nr * ai) / den
    lbr_ref[...] = lb_re
    lbi_ref[...] = lb_im
    cr = cr_ref[...]
    ci = ci_ref[...]
    fr = f_re[:, None, :]
    fi = f_im[:, None, :]
    cf_re = (cr * fr - ci * fi).reshape(G * H, P)
    cf_im_neg = (-(cr * fi + ci * fr)).reshape(G * H, P)

    spread = (lax.broadcasted_iota(jnp.int32, (P, CLUSTER_STATE), 1) % P
              == lax.broadcasted_iota(jnp.int32, (P, CLUSTER_STATE), 0)).astype(_BF16)
    own_group = (lax.broadcasted_iota(jnp.int32, (CLUSTER_IN, CLUSTER_STATE), 0) // H
                 == lax.broadcasted_iota(jnp.int32, (CLUSTER_IN, CLUSTER_STATE), 1) // P)

    def block_diag(rows_gh_p):
        return jnp.where(own_group, _dot(rows_gh_p.astype(_BF16), spread), 0.0)

    for c in range(N_CLUSTERS):
        rows = slice(c * CLUSTER_IN, (c + 1) * CLUSTER_IN)
        for part, (bt_ref, cf) in enumerate(((btr_ref, cf_re), (bti_ref, cf_im_neg))):
            cols = slice(part * CLUSTER_STATE, (part + 1) * CLUSTER_STATE)
            bbd_ref[c, :, cols] = block_diag(bt_ref[rows, :]).astype(bbd_ref.dtype)
            cbd_ref[c, cols, :] = block_diag(cf[rows, :]).T.astype(cbd_ref.dtype)


def _s5_prep(a_re, a_im, log_dt, b_re, b_im, c_re, c_im):
    depth = a_re.shape[0]
    G, P, H, C = N_SSM_GROUPS, SSM_STATE, SSM_GROUP_WIDTH, N_CLUSTERS
    gp = pl.BlockSpec((None, G, P), lambda i: (i, 0, 0))
    g1 = pl.BlockSpec((None, G, 1), lambda i: (i, 0, 0))
    gh_p = pl.BlockSpec((None, G * H, P), lambda i: (i, 0, 0))
    ghp = pl.BlockSpec((None, G, H, P), lambda i: (i, 0, 0, 0))
    rows_gh = lambda b: jnp.swapaxes(b, 2, 3).reshape(depth, G * H, P)
    lb_re, lb_im, bbd, cbd = pl.pallas_call(
        _s5_prep_kernel,
        grid=(depth,),
        in_specs=[gp, gp, g1, gh_p, gh_p, ghp, ghp],
        out_specs=[gp, gp,
                   pl.BlockSpec((None, C, CLUSTER_IN, 2 * CLUSTER_STATE), lambda i: (i, 0, 0, 0)),
                   pl.BlockSpec((None, C, 2 * CLUSTER_STATE, CLUSTER_IN), lambda i: (i, 0, 0, 0))],
        out_shape=[jax.ShapeDtypeStruct((depth, G, P), _F32)] * 2
        + [jax.ShapeDtypeStruct((depth, C, CLUSTER_IN, 2 * CLUSTER_STATE), _BF16),
           jax.ShapeDtypeStruct((depth, C, 2 * CLUSTER_STATE, CLUSTER_IN), _BF16)],
        name="s5_prep",
    )(a_re, a_im, log_dt.reshape(depth, G, 1), rows_gh(b_re), rows_gh(b_im), c_re, c_im)
    lam = jnp.stack([lb_re.reshape(depth, C, CLUSTER_STATE), lb_im.reshape(depth, C, CLUSTER_STATE)],
                    axis=2)
    return lam, bbd, cbd


SCAN_SLICES = 4


def _s5_phases(first_step, u_ref, lam_ref, bbd_ref, cbd_ref, d_ref, wglu_ref, g_ref, o_ref,
               us_ref, ys_ref, bu_ref, st_ref, *, chunk, batch):
    n_slab = SSM_WIDTH // LANES
    slabs_per_cluster = CLUSTER_IN // LANES

    @pl.when(first_step)
    def _():
        st_ref[...] = jnp.zeros_like(st_ref)

    for b in range(batch):
        for j in range(n_slab):
            lo = b * SSM_WIDTH + j * LANES
            us_ref[j, pl.ds(b, chunk, stride=batch), :] = u_ref[:, lo:lo + LANES].astype(_F32)
    for c in range(N_CLUSTERS):
        u_c = jnp.concatenate(
            [us_ref[c * slabs_per_cluster + j] for j in range(slabs_per_cluster)], axis=1)
        bu_ref[c] = _dot(u_c.astype(_BF16), bbd_ref[c])
    yield

    ys = []
    for c in range(N_CLUSTERS):
        lr = jnp.broadcast_to(lam_ref[c, 0:1, :], (batch, CLUSTER_STATE))
        li = jnp.broadcast_to(lam_ref[c, 1:2, :], (batch, CLUSTER_STATE))
        sr, si = st_ref[c, 0], st_ref[c, 1]
        for t in range(chunk):
            r0 = t * batch
            br = bu_ref[c, r0:r0 + batch, 0:CLUSTER_STATE]
            bi = bu_ref[c, r0:r0 + batch, CLUSTER_STATE:2 * CLUSTER_STATE]
            sr, si = lr * sr - li * si + br, lr * si + li * sr + bi
            bu_ref[c, r0:r0 + batch, 0:CLUSTER_STATE] = sr
            bu_ref[c, r0:r0 + batch, CLUSTER_STATE:2 * CLUSTER_STATE] = si
            if (t + 1) % (chunk // SCAN_SLICES) == 0:
                yield
        st_ref[c, 0] = sr
        st_ref[c, 1] = si
        ys.append(_dot(bu_ref[c].astype(_BF16), cbd_ref[c]))
        yield
    u_all = jnp.concatenate([us_ref[j] for j in range(n_slab)], axis=1)
    y = jnp.concatenate(ys, axis=1) + d_ref[...] * u_all
    y = _gelu_tanh(y)
    z = _dot(y.astype(_BF16), wglu_ref[...])
    yield
    y = _rms(y * _sigmoid(z), g_ref[...])
    for j in range(n_slab):
        ys_ref[j] = y[:, j * LANES:(j + 1) * LANES]
    for b in range(batch):
        for j in range(n_slab):
            lo = b * SSM_WIDTH + j * LANES
            o_ref[:, lo:lo + LANES] = ys_ref[j, pl.ds(b, chunk, stride=batch), :].astype(o_ref.dtype)


def _attn_phases(first_tile, sink_ref, q_ref, kc_ref, kp_ref, vc_ref, vp_ref, g_ref, o_ref, *,
                 n_blocks):
    gq = N_Q_HEADS // N_KV_HEADS
    rows = gq * BLOCK
    qpos = lax.broadcasted_iota(jnp.int32, (rows, BLOCK), 0) % BLOCK
    kidx = lax.broadcasted_iota(jnp.int32, (rows, BLOCK), 1)
    own = kidx <= qpos
    head_of_row = lax.broadcasted_iota(jnp.int32, (rows, 1), 0) // BLOCK
    low_kv = lax.broadcasted_iota(jnp.int32, (2 * BLOCK, KV_WIDTH), 1) < HEAD_DIM
    low_out = lax.broadcasted_iota(jnp.int32, (rows, LANES), 1) < HEAD_DIM
    g = g_ref[...]

    sink_cols = []
    for kv in range(N_KV_HEADS):
        col = jnp.full((rows, 1), sink_ref[0, kv * gq], _F32)
        for j in range(1, gq):
            col = jnp.where(head_of_row == j, sink_ref[0, kv * gq + j], col)
        sink_cols.append(col * LOG2_E)

    for blk in range(n_blocks):
        lo, hi = blk * BLOCK, (blk + 1) * BLOCK
        if blk == 0:
            k_prev, v_prev = kp_ref[...], vp_ref[...]
        else:
            k_prev, v_prev = kc_ref[lo - BLOCK:lo, :], vc_ref[lo - BLOCK:lo, :]
        keys = jnp.concatenate([k_prev, kc_ref[lo:hi, :]], axis=0)
        vals = jnp.concatenate([v_prev, vc_ref[lo:hi, :]], axis=0)
        zero = jnp.zeros_like(keys)
        q_all = jnp.concatenate([q_ref[lo:hi, j * LANES:(j + 1) * LANES] for j in range(gq)], axis=0)
        acc = None
        invs = []
        for kv in range(N_KV_HEADS):
            sel = low_kv if kv == 0 else jnp.logical_not(low_kv)
            k_sel = jnp.where(sel, keys, zero)
            v_sel = jnp.where(sel, vals, zero)
            s = lax.dot_general(q_all, k_sel, (((1,), (1,)), ((), ())),
                                preferred_element_type=_F32)
            s_prev = s[:, :BLOCK]
            if blk == 0:
                s_prev = jnp.where(first_tile, -jnp.inf, s_prev)
            s = jnp.where(own, s[:, BLOCK:], s_prev)
            sink = sink_cols[kv]
            m = jnp.maximum(jnp.max(s, axis=1, keepdims=True), sink)
            e = jnp.exp2(s - m)
            den = jnp.sum(e, axis=1, keepdims=True) + jnp.exp2(sink - m)
            invs.append(1.0 / den)
            p = jnp.concatenate([jnp.where(own, 0.0, e), jnp.where(own, e, 0.0)], axis=1)
            pv = _dot(p.astype(_BF16), v_sel)
            acc = pv if acc is None else acc + pv
        out = acc * jnp.where(low_out, invs[0], invs[1])
        outs = [out[j * BLOCK:(j + 1) * BLOCK, :] for j in range(gq)]
        ms = sum(jnp.sum(o * o, axis=1, keepdims=True) for o in outs) * (1.0 / ATTN_WIDTH)
        scale = lax.rsqrt(ms + RMS_EPS)
        for j in range(gq):
            o_ref[lo:hi, j * LANES:(j + 1) * LANES] = (
                outs[j] * scale * g[:, j * LANES:(j + 1) * LANES]).astype(o_ref.dtype)
        yield


N_S5_IN = 7
N_ATTN_IN = 7
ATTN_AFTER_S5_PHASE = (0, 3, 6, 10)


def _mixer_kernel(*refs, chunk, batch, n_blocks, tiles_per_seq):
    s5_in = refs[:N_S5_IN]
    attn_in = refs[N_S5_IN:N_S5_IN + N_ATTN_IN]
    ssm_ref, attn_ref = refs[N_S5_IN + N_ATTN_IN:N_S5_IN + N_ATTN_IN + 2]
    scratch = refs[N_S5_IN + N_ATTN_IN + 2:]
    step = pl.program_id(0)
    s5 = _s5_phases(step == 0, *s5_in, ssm_ref, *scratch, chunk=chunk, batch=batch)
    attn = _attn_phases(step % tiles_per_seq == 0, *attn_in, attn_ref, n_blocks=n_blocks)
    for phase, _ in enumerate(s5):
        if phase in ATTN_AFTER_S5_PHASE:
            next(attn)
    for _ in attn:
        pass


def _mixer(u_tm, q_tm, k_tm, v_tm, s5w, sinks, g_attn, layer, *, batch, seq):
    chunk = S5_CHUNK
    rows = chunk * batch
    tile = rows
    n_blocks = tile // BLOCK
    tiles_per_seq = seq // tile
    n_slab = SSM_WIDTH // LANES
    const = lambda shape: pl.BlockSpec((None,) + shape, lambda i: (layer,) + (0,) * len(shape))
    cur = lambda w: pl.BlockSpec((tile, w), lambda i: (i % tiles_per_seq, i // tiles_per_seq))
    prev = lambda w: pl.BlockSpec(
        (BLOCK, w),
        lambda i: (jnp.maximum((i % tiles_per_seq) * n_blocks - 1, 0), i // tiles_per_seq))
    return pl.pallas_call(
        functools.partial(_mixer_kernel, chunk=chunk, batch=batch, n_blocks=n_blocks,
                          tiles_per_seq=tiles_per_seq),
        grid=(seq // chunk,),
        in_specs=[
            pl.BlockSpec((chunk, batch * SSM_WIDTH), lambda i: (i, 0)),
            const((N_CLUSTERS, 2, CLUSTER_STATE)),
            const((N_CLUSTERS, CLUSTER_IN, 2 * CLUSTER_STATE)),
            const((N_CLUSTERS, 2 * CLUSTER_STATE, CLUSTER_IN)),
            const((1, SSM_WIDTH)),
            const((SSM_WIDTH, SSM_WIDTH)),
            const((1, SSM_WIDTH)),
            pl.BlockSpec((None, 1, N_Q_HEADS), lambda i: (layer, 0, 0), memory_space=pltpu.SMEM),
            cur(ATTN_WIDTH), cur(KV_WIDTH), prev(KV_WIDTH), cur(KV_WIDTH), prev(KV_WIDTH),
            const((1, ATTN_WIDTH)),
        ],
        out_specs=[pl.BlockSpec((chunk, batch * SSM_WIDTH), lambda i: (i, 0)), cur(ATTN_WIDTH)],
        out_shape=[jax.ShapeDtypeStruct((seq, batch * SSM_WIDTH), _BF16),
                   jax.ShapeDtypeStruct((seq, batch * ATTN_WIDTH), _BF16)],
        scratch_shapes=[
            pltpu.VMEM((n_slab, rows, LANES), _F32),
            pltpu.VMEM((n_slab, rows, LANES), _F32),
            pltpu.VMEM((N_CLUSTERS, rows, 2 * CLUSTER_STATE), _F32),
            pltpu.VMEM((N_CLUSTERS, 2, batch, CLUSTER_STATE), _F32),
        ],
        compiler_params=pltpu.CompilerParams(
            dimension_semantics=("arbitrary",), vmem_limit_bytes=VMEM_LIMIT),
        name="mixer",
    )(u_tm, s5w["lam"], s5w["bbd"], s5w["cbd"], s5w["d_skip"], s5w["w_glu"], s5w["g_ssm"],
      sinks, q_tm, k_tm, k_tm, v_tm, v_tm, g_attn)


def _in_proj(h, g_mix, w_in_ref, u_ref, q_ref, k_ref, v_ref):
    hn = _rms(h, g_mix).astype(_BF16)
    uq_width = SSM_WIDTH + ATTN_WIDTH
    uq = _dot(hn, w_in_ref[:, :uq_width])
    u_ref[...] = uq[:, :SSM_WIDTH].astype(u_ref.dtype)
    q_ref[...] = (uq[:, SSM_WIDTH:] * Q_SCALE).astype(q_ref.dtype)
    half = hn.shape[0] // 2
    for rows in (slice(0, half), slice(half, 2 * half)):
        kv = _dot(hn[rows, :], w_in_ref[:, uq_width:])
        k_ref[rows, :] = kv[:, :KV_WIDTH].astype(k_ref.dtype)
        v_ref[rows, :] = kv[:, KV_WIDTH:].astype(v_ref.dtype)


def _pre_kernel(h_ref, g_mix_ref, w_in_ref, u_ref, q_ref, k_ref, v_ref):
    _in_proj(h_ref[...], g_mix_ref[...], w_in_ref, u_ref, q_ref, k_ref, v_ref)


def _mid_kernel(*refs, last):
    (h_ref, ssm_ref, attn_ref, p_ref, w_out_ref, g_ffn_ref, w_fi_ref, w_fo_ref, g_ple_ref,
     w_pg_ref, w_pp_ref) = refs[:11]
    if last:
        g_fin_ref, out_ref = refs[11:]
    else:
        g_mix_ref, w_in_ref, h_out_ref, u_ref, q_ref, k_ref, v_ref = refs[11:]

    res_ref = out_ref if last else h_out_ref
    res_ref[...] = (h_ref[...] + _dot(ssm_ref[...], w_out_ref[:SSM_WIDTH, :])
                    + _dot(attn_ref[...], w_out_ref[SSM_WIDTH:, :]))
    hn = _rms(res_ref[...], g_ffn_ref[...]).astype(_BF16)
    for c in range(FFN_HIDDEN // FFN_CHUNK):
        lo = c * FFN_CHUNK
        gate = _dot(hn, w_fi_ref[:, lo:lo + FFN_CHUNK])
        up = _dot(hn, w_fi_ref[:, FFN_HIDDEN + lo:FFN_HIDDEN + lo + FFN_CHUNK])
        act = (gate * _sigmoid(gate) * up).astype(_BF16)
        res_ref[...] += _dot(act, w_fo_ref[lo:lo + FFN_CHUNK, :])
    h = res_ref[...]
    gate = _sigmoid(_dot(_rms(h, g_ple_ref[...]).astype(_BF16), w_pg_ref[...]))
    h = h + gate * _dot(p_ref[...].astype(_BF16), w_pp_ref[...])
    if last:
        out_ref[...] = _rms(h, g_fin_ref[...])
    else:
        h_out_ref[...] = h
        _in_proj(h, g_mix_ref[...], w_in_ref, u_ref, q_ref, k_ref, v_ref)


def _layer_const(layer, shape):
    return pl.BlockSpec((None,) + shape, lambda b, i: (layer,) + (0,) * len(shape),
                        pipeline_mode=pl.Buffered(1))


def _proj_out_specs(tile, batch, seq):
    tm = lambda w: pl.BlockSpec((tile, w), lambda b, i: (i, b))
    specs = [tm(SSM_WIDTH), tm(ATTN_WIDTH), tm(KV_WIDTH), tm(KV_WIDTH)]
    shapes = [jax.ShapeDtypeStruct((seq, batch * w), _BF16)
              for w in (SSM_WIDTH, ATTN_WIDTH, KV_WIDTH, KV_WIDTH)]
    return specs, shapes


def _pre(x2d, g_mix, w_in, *, batch, seq):
    tile = PRE_TILE
    n_t = seq // tile
    specs, shapes = _proj_out_specs(tile, batch, seq)
    return pl.pallas_call(
        _pre_kernel,
        grid=(batch, n_t),
        in_specs=[
            pl.BlockSpec((tile, D_MODEL), lambda b, i: (b * n_t + i, 0)),
            _layer_const(0, (1, D_MODEL)),
            _layer_const(0, (D_MODEL, IN_WIDTH)),
        ],
        out_specs=specs,
        out_shape=shapes,
        compiler_params=pltpu.CompilerParams(
            dimension_semantics=("parallel", "parallel"), vmem_limit_bytes=VMEM_LIMIT),
        name="pre_in_proj",
    )(x2d, g_mix, w_in)


def _mid(h2d, ssm_tm, attn_tm, p2d, weights, layer, *, batch, seq, depth, h_batch_major):
    tile = TOKEN_TILE
    n_t = seq // tile
    last = layer == depth - 1
    bm = lambda w, off=0: pl.BlockSpec((tile, w), lambda b, i: (off + b * n_t + i, 0))
    tm = lambda w: pl.BlockSpec((tile, w), lambda b, i: (i, b))
    in_specs = [
        bm(D_MODEL) if h_batch_major else tm(D_MODEL),
        tm(SSM_WIDTH), tm(ATTN_WIDTH),
        bm(PLE_DIM, layer * batch * n_t),
        _layer_const(layer, (D_MODEL, D_MODEL)),
        _layer_const(layer, (1, D_MODEL)),
        _layer_const(layer, (D_MODEL, 2 * FFN_HIDDEN)),
        _layer_const(layer, (FFN_HIDDEN, D_MODEL)),
        _layer_const(layer, (1, D_MODEL)),
        _layer_const(layer, (D_MODEL, D_MODEL)),
        _layer_const(layer, (PLE_DIM, D_MODEL)),
    ]
    args = [h2d, ssm_tm, attn_tm, p2d, weights["w_out"], weights["g_ffn"], weights["w_fi"],
            weights["w_fo"], weights["g_ple"], weights["w_pg"], weights["w_pp"]]
    if last:
        in_specs.append(pl.BlockSpec((1, D_MODEL), lambda b, i: (0, 0)))
        args.append(weights["g_fin"])
        out_specs = [bm(D_MODEL)]
        out_shape = [jax.ShapeDtypeStruct((batch * seq, D_MODEL), _F32)]
    else:
        in_specs += [_layer_const(layer + 1, (1, D_MODEL)),
                     _layer_const(layer + 1, (D_MODEL, IN_WIDTH))]
        args += [weights["g_mix"], weights["w_in"]]
        specs, shapes = _proj_out_specs(tile, batch, seq)
        out_specs = [tm(D_MODEL)] + specs
        out_shape = [jax.ShapeDtypeStruct((seq, batch * D_MODEL), _F32)] + shapes
    return pl.pallas_call(
        functools.partial(_mid_kernel, last=last),
        grid=(batch, n_t),
        in_specs=in_specs,
        out_specs=out_specs,
        out_shape=out_shape,
        compiler_params=pltpu.CompilerParams(
            dimension_semantics=("parallel", "parallel"), vmem_limit_bytes=VMEM_LIMIT),
        name="mid_last" if last else "mid",
    )(*args)


def _pair_heads(a, axis):
    gq = N_Q_HEADS // N_KV_HEADS
    shape = a.shape
    a = a.reshape(shape[:axis] + (N_KV_HEADS, gq, HEAD_DIM) + shape[axis + 1:])
    return jnp.swapaxes(a, axis, axis + 1).reshape(shape)


def kernel(x, p, norm_mix, w_in, ssm_a_re, ssm_a_im, ssm_log_dt, ssm_b_re, ssm_b_im, ssm_c_re,
           ssm_c_im, ssm_d, ssm_w_glu, attn_sinks, norm_ssm_out, norm_attn_out, w_out, norm_ffn,
           w_ffn_in, w_ffn_out, norm_ple, w_ple_gate, w_ple_proj, norm_final):
    batch, seq, _ = x.shape
    depth = w_in.shape[0]
    assert batch == SUBLANES and seq % max(TOKEN_TILE, PRE_TILE, S5_CHUNK * batch) == 0

    row = lambda a: a.reshape(a.shape[0], 1, a.shape[-1])
    q_lo, q_hi = SSM_WIDTH, SSM_WIDTH + ATTN_WIDTH
    weights = {
        "g_mix": row(norm_mix),
        "w_in": jnp.concatenate(
            [w_in[:, :, :q_lo], _pair_heads(w_in[:, :, q_lo:q_hi], 2), w_in[:, :, q_hi:]],
            axis=2).astype(_BF16),
        "w_out": jnp.concatenate(
            [w_out[:, :SSM_WIDTH, :], _pair_heads(w_out[:, SSM_WIDTH:, :], 1)],
            axis=1).astype(_BF16),
        "g_ffn": row(norm_ffn),
        "w_fi": w_ffn_in.astype(_BF16),
        "w_fo": w_ffn_out.astype(_BF16),
        "g_ple": row(norm_ple),
        "w_pg": w_ple_gate.astype(_BF16),
        "w_pp": w_ple_proj.astype(_BF16),
        "g_fin": norm_final.reshape(1, D_MODEL),
    }
    g_attn = row(_pair_heads(norm_attn_out, 1))
    sinks = row(attn_sinks)

    lam, bbd, cbd = _s5_prep(ssm_a_re, ssm_a_im, ssm_log_dt, ssm_b_re, ssm_b_im, ssm_c_re, ssm_c_im)
    s5w = {"lam": lam, "bbd": bbd, "cbd": cbd, "d_skip": row(ssm_d),
           "w_glu": ssm_w_glu.astype(_BF16), "g_ssm": row(norm_ssm_out)}

    p2d = p.reshape(depth * batch * seq, PLE_DIM)
    h2d = x.reshape(batch * seq, D_MODEL)
    u, q, k, v = _pre(h2d, weights["g_mix"], weights["w_in"], batch=batch, seq=seq)
    for layer in range(depth):
        ssm, attn = _mixer(u, q, k, v, s5w, sinks, g_attn, layer, batch=batch, seq=seq)
        outs = _mid(h2d, ssm, attn, p2d, weights, layer,
                    batch=batch, seq=seq, depth=depth, h_batch_major=(layer == 0))
        if layer == depth - 1:
            return outs[0].reshape(batch, seq, D_MODEL)
        h2d, u, q, k, v = outs
```

```python
import functools
import math

import jax
import jax.numpy as jnp
from jax import lax
from jax.experimental import pallas as pl
from jax.experimental.pallas import tpu as pltpu

D_MODEL = 1024
SSM_WIDTH = 512
SSM_GROUP_WIDTH = 16
N_SSM_GROUPS = 32
SSM_STATE = 64
HEAD_DIM = 64
N_Q_HEADS = 8
N_KV_HEADS = 2
ATTN_WIDTH = 512
KV_WIDTH = 128
IN_WIDTH = 1280
WINDOW = 128
BLOCK = 128
FFN_HIDDEN = 2816
PLE_DIM = 256
RMS_EPS = 1e-6

SUBLANES = 8
LANES = 128
MXU_DIM = 256

CLUSTER_GROUPS = MXU_DIM // SSM_GROUP_WIDTH
N_CLUSTERS = N_SSM_GROUPS // CLUSTER_GROUPS
CLUSTER_IN = CLUSTER_GROUPS * SSM_GROUP_WIDTH
CLUSTER_STATE = CLUSTER_GROUPS * SSM_STATE

TOKEN_TILE = 1024
PRE_TILE = 1024
S5_CHUNK = 64
FFN_CHUNK = 256
VMEM_LIMIT = 60 * 1024 * 1024
LOG2_E = math.log2(math.e)
Q_SCALE = HEAD_DIM ** -0.5 * LOG2_E

_F32 = jnp.float32
_BF16 = jnp.bfloat16


def _dot(a, b):
    return jnp.dot(a, b, preferred_element_type=_F32)


def _rms(x, g):
    ms = jnp.mean(x * x, axis=-1, keepdims=True)
    return x * lax.rsqrt(ms + RMS_EPS) * g


def _sigmoid(x):
    return 1.0 / (1.0 + jnp.exp2(x * (-LOG2_E)))


def _gelu_tanh(x):
    k = -2.0 * LOG2_E * math.sqrt(2.0 / math.pi)
    t = x * (x * x * (k * 0.044715) + k)
    return x / (1.0 + jnp.exp2(t))


def _s5_prep_kernel(ar_ref, ai_ref, ldt_ref, btr_ref, bti_ref, cr_ref, ci_ref,
                    lbr_ref, lbi_ref, bbd_ref, cbd_ref):
    G, H, P = N_SSM_GROUPS, SSM_GROUP_WIDTH, SSM_STATE
    ar = ar_ref[...]
    ai = ai_ref[...]
    dt = jnp.exp(ldt_ref[...])
    mag = jnp.exp(ar * dt)
    lb_re = mag * jnp.cos(ai * dt)
    lb_im = mag * jnp.sin(ai * dt)
    den = ar * ar + ai * ai
    nr = lb_re - 1.0
    ni = lb_im
    f_re = (nr * ar + ni * ai) / den
    f_im = (ni * ar - nr * ai) / den
    lbr_ref[...] = lb_re
    lbi_ref[...] = lb_im
    cr = cr_ref[...]
    ci = ci_ref[...]
    fr = f_re[:, None, :]
    fi = f_im[:, None, :]
    cf_re = (cr * fr - ci * fi).reshape(G * H, P)
    cf_im_neg = (-(cr * fi + ci * fr)).reshape(G * H, P)

    spread = (lax.broadcasted_iota(jnp.int32, (P, CLUSTER_STATE), 1) % P
              == lax.broadcasted_iota(jnp.int32, (P, CLUSTER_STATE), 0)).astype(_BF16)
    own_group = (lax.broadcasted_iota(jnp.int32, (CLUSTER_IN, CLUSTER_STATE), 0) // H
                 == lax.broadcasted_iota(jnp.int32, (CLUSTER_IN, CLUSTER_STATE), 1) // P)

    def block_diag(rows_gh_p):
        return jnp.where(own_group, _dot(rows_gh_p.astype(_BF16), spread), 0.0)

    for c in range(N_CLUSTERS):
        rows = slice(c * CLUSTER_IN, (c + 1) * CLUSTER_IN)
        for part, (bt_ref, cf) in enumerate(((btr_ref, cf_re), (bti_ref, cf_im_neg))):
            cols = slice(part * CLUSTER_STATE, (part + 1) * CLUSTER_STATE)
            bbd_ref[c, :, cols] = block_diag(bt_ref[rows, :]).astype(bbd_ref.dtype)
            cbd_ref[c, cols, :] = block_diag(cf[rows, :]).T.astype(cbd_ref.dtype)


def _s5_prep(a_re, a_im, log_dt, b_re, b_im, c_re, c_im):
    depth = a_re.shape[0]
    G, P, H, C = N_SSM_GROUPS, SSM_STATE, SSM_GROUP_WIDTH, N_CLUSTERS
    gp = pl.BlockSpec((None, G, P), lambda i: (i, 0, 0))
    g1 = pl.BlockSpec((None, G, 1), lambda i: (i, 0, 0))
    gh_p = pl.BlockSpec((None, G * H, P), lambda i: (i, 0, 0))
    ghp = pl.BlockSpec((None, G, H, P), lambda i: (i, 0, 0, 0))
    rows_gh = lambda b: jnp.swapaxes(b, 2, 3).reshape(depth, G * H, P)
    lb_re, lb_im, bbd, cbd = pl.pallas_call(
        _s5_prep_kernel,
        grid=(depth,),
        in_specs=[gp, gp, g1, gh_p, gh_p, ghp, ghp],
        out_specs=[gp, gp,
                   pl.BlockSpec((None, C, CLUSTER_IN, 2 * CLUSTER_STATE), lambda i: (i, 0, 0, 0)),
                   pl.BlockSpec((None, C, 2 * CLUSTER_STATE, CLUSTER_IN), lambda i: (i, 0, 0, 0))],
        out_shape=[jax.ShapeDtypeStruct((depth, G, P), _F32)] * 2
        + [jax.ShapeDtypeStruct((depth, C, CLUSTER_IN, 2 * CLUSTER_STATE), _BF16),
           jax.ShapeDtypeStruct((depth, C, 2 * CLUSTER_STATE, CLUSTER_IN), _BF16)],
        name="s5_prep",
    )(a_re, a_im, log_dt.reshape(depth, G, 1), rows_gh(b_re), rows_gh(b_im), c_re, c_im)
    lam = jnp.stack([lb_re.reshape(depth, C, CLUSTER_STATE), lb_im.reshape(depth, C, CLUSTER_STATE)],
                    axis=2)
    return lam, bbd, cbd


SCAN_SLICES = 4


def _s5_phases(first_step, u_ref, lam_ref, bbd_ref, cbd_ref, d_ref, o_ref,
               us_ref, ys_ref, bu_ref, st_ref, *, chunk, batch):
    n_slab = SSM_WIDTH // LANES
    slabs_per_cluster = CLUSTER_IN // LANES

    @pl.when(first_step)
    def _():
        st_ref[...] = jnp.zeros_like(st_ref)

    for b in range(batch):
        for j in range(n_slab):
            lo = b * SSM_WIDTH + j * LANES
            us_ref[j, pl.ds(b, chunk, stride=batch), :] = u_ref[:, lo:lo + LANES].astype(_F32)
    for c in range(N_CLUSTERS):
        u_c = jnp.concatenate(
            [us_ref[c * slabs_per_cluster + j] for j in range(slabs_per_cluster)], axis=1)
        bu_ref[c] = _dot(u_c.astype(_BF16), bbd_ref[c])
    yield

    ys = []
    for c in range(N_CLUSTERS):
        lr = jnp.broadcast_to(lam_ref[c, 0:1, :], (batch, CLUSTER_STATE))
        li = jnp.broadcast_to(lam_ref[c, 1:2, :], (batch, CLUSTER_STATE))
        sr, si = st_ref[c, 0], st_ref[c, 1]
        for t in range(chunk):
            r0 = t * batch
            br = bu_ref[c, r0:r0 + batch, 0:CLUSTER_STATE]
            bi = bu_ref[c, r0:r0 + batch, CLUSTER_STATE:2 * CLUSTER_STATE]
            sr, si = lr * sr - li * si + br, lr * si + li * sr + bi
            bu_ref[c, r0:r0 + batch, 0:CLUSTER_STATE] = sr
            bu_ref[c, r0:r0 + batch, CLUSTER_STATE:2 * CLUSTER_STATE] = si
            if (t + 1) % (chunk // SCAN_SLICES) == 0:
                yield
        st_ref[c, 0] = sr
        st_ref[c, 1] = si
        ys.append(_dot(bu_ref[c].astype(_BF16), cbd_ref[c]))
        yield
    u_all = jnp.concatenate([us_ref[j] for j in range(n_slab)], axis=1)
    y = jnp.concatenate(ys, axis=1) + d_ref[...] * u_all
    yield
    for j in range(n_slab):
        ys_ref[j] = y[:, j * LANES:(j + 1) * LANES]
    for b in range(batch):
        for j in range(n_slab):
            lo = b * SSM_WIDTH + j * LANES
            o_ref[:, lo:lo + LANES] = ys_ref[j, pl.ds(b, chunk, stride=batch), :].astype(o_ref.dtype)


def _attn_phases(first_tile, sink_ref, q_ref, kc_ref, kp_ref, vc_ref, vp_ref, g_ref, o_ref, *,
                 n_blocks):
    gq = N_Q_HEADS // N_KV_HEADS
    rows = gq * BLOCK
    qpos = lax.broadcasted_iota(jnp.int32, (rows, BLOCK), 0) % BLOCK
    kidx = lax.broadcasted_iota(jnp.int32, (rows, BLOCK), 1)
    own = kidx <= qpos
    head_of_row = lax.broadcasted_iota(jnp.int32, (rows, 1), 0) // BLOCK
    low_kv = lax.broadcasted_iota(jnp.int32, (2 * BLOCK, KV_WIDTH), 1) < HEAD_DIM
    low_out = lax.broadcasted_iota(jnp.int32, (rows, LANES), 1) < HEAD_DIM
    g = g_ref[...]

    sink_cols = []
    for kv in range(N_KV_HEADS):
        col = jnp.full((rows, 1), sink_ref[0, kv * gq], _F32)
        for j in range(1, gq):
            col = jnp.where(head_of_row == j, sink_ref[0, kv * gq + j], col)
        sink_cols.append(col * LOG2_E)

    for blk in range(n_blocks):
        lo, hi = blk * BLOCK, (blk + 1) * BLOCK
        if blk == 0:
            k_prev, v_prev = kp_ref[...], vp_ref[...]
        else:
            k_prev, v_prev = kc_ref[lo - BLOCK:lo, :], vc_ref[lo - BLOCK:lo, :]
        keys = jnp.concatenate([k_prev, kc_ref[lo:hi, :]], axis=0)
        vals = jnp.concatenate([v_prev, vc_ref[lo:hi, :]], axis=0)
        zero = jnp.zeros_like(keys)
        q_all = jnp.concatenate([q_ref[lo:hi, j * LANES:(j + 1) * LANES] for j in range(gq)], axis=0)
        acc = None
        invs = []
        for kv in range(N_KV_HEADS):
            sel = low_kv if kv == 0 else jnp.logical_not(low_kv)
            k_sel = jnp.where(sel, keys, zero)
            v_sel = jnp.where(sel, vals, zero)
            s = lax.dot_general(q_all, k_sel, (((1,), (1,)), ((), ())),
                                preferred_element_type=_F32)
            s_prev = s[:, :BLOCK]
            if blk == 0:
                s_prev = jnp.where(first_tile, -jnp.inf, s_prev)
            s = jnp.where(own, s[:, BLOCK:], s_prev)
            sink = sink_cols[kv]
            m = jnp.maximum(jnp.max(s, axis=1, keepdims=True), sink)
            e = jnp.exp2(s - m)
            den = jnp.sum(e, axis=1, keepdims=True) + jnp.exp2(sink - m)
            invs.append(1.0 / den)
            p = jnp.concatenate([jnp.where(own, 0.0, e), jnp.where(own, e, 0.0)], axis=1)
            pv = _dot(p.astype(_BF16), v_sel)
            acc = pv if acc is None else acc + pv
        out = acc * jnp.where(low_out, invs[0], invs[1])
        outs = [out[j * BLOCK:(j + 1) * BLOCK, :] for j in range(gq)]
        ms = sum(jnp.sum(o * o, axis=1, keepdims=True) for o in outs) * (1.0 / ATTN_WIDTH)
        scale = lax.rsqrt(ms + RMS_EPS)
        for j in range(gq):
            o_ref[lo:hi, j * LANES:(j + 1) * LANES] = (
                outs[j] * scale * g[:, j * LANES:(j + 1) * LANES]).astype(o_ref.dtype)
        yield


N_S5_IN = 5
N_ATTN_IN = 7
ATTN_AFTER_S5_PHASE = (0, 3, 6, 10)


def _mixer_kernel(*refs, chunk, batch, n_blocks, tiles_per_seq):
    s5_in = refs[:N_S5_IN]
    attn_in = refs[N_S5_IN:N_S5_IN + N_ATTN_IN]
    ssm_ref, attn_ref = refs[N_S5_IN + N_ATTN_IN:N_S5_IN + N_ATTN_IN + 2]
    scratch = refs[N_S5_IN + N_ATTN_IN + 2:]
    step = pl.program_id(0)
    s5 = _s5_phases(step == 0, *s5_in, ssm_ref, *scratch, chunk=chunk, batch=batch)
    attn = _attn_phases(step % tiles_per_seq == 0, *attn_in, attn_ref, n_blocks=n_blocks)
    for phase, _ in enumerate(s5):
        if phase in ATTN_AFTER_S5_PHASE:
            next(attn)
    for _ in attn:
        pass


def _mixer(u_tm, q_tm, k_tm, v_tm, s5w, sinks, g_attn, layer, *, batch, seq):
    chunk = S5_CHUNK
    rows = chunk * batch
    tile = rows
    n_blocks = tile // BLOCK
    tiles_per_seq = seq // tile
    n_slab = SSM_WIDTH // LANES
    const = lambda shape: pl.BlockSpec((None,) + shape, lambda i: (layer,) + (0,) * len(shape))
    cur = lambda w: pl.BlockSpec((tile, w), lambda i: (i % tiles_per_seq, i // tiles_per_seq))
    prev = lambda w: pl.BlockSpec(
        (BLOCK, w),
        lambda i: (jnp.maximum((i % tiles_per_seq) * n_blocks - 1, 0), i // tiles_per_seq))
    return pl.pallas_call(
        functools.partial(_mixer_kernel, chunk=chunk, batch=batch, n_blocks=n_blocks,
                          tiles_per_seq=tiles_per_seq),
        grid=(seq // chunk,),
        in_specs=[
            pl.BlockSpec((chunk, batch * SSM_WIDTH), lambda i: (i, 0)),
            const((N_CLUSTERS, 2, CLUSTER_STATE)),
            const((N_CLUSTERS, CLUSTER_IN, 2 * CLUSTER_STATE)),
            const((N_CLUSTERS, 2 * CLUSTER_STATE, CLUSTER_IN)),
            const((1, SSM_WIDTH)),
            pl.BlockSpec((None, 1, N_Q_HEADS), lambda i: (layer, 0, 0), memory_space=pltpu.SMEM),
            cur(ATTN_WIDTH), cur(KV_WIDTH), prev(KV_WIDTH), cur(KV_WIDTH), prev(KV_WIDTH),
            const((1, ATTN_WIDTH)),
        ],
        out_specs=[pl.BlockSpec((chunk, batch * SSM_WIDTH), lambda i: (i, 0)), cur(ATTN_WIDTH)],
        out_shape=[jax.ShapeDtypeStruct((seq, batch * SSM_WIDTH), _BF16),
                   jax.ShapeDtypeStruct((seq, batch * ATTN_WIDTH), _BF16)],
        scratch_shapes=[
            pltpu.VMEM((n_slab, rows, LANES), _F32),
            pltpu.VMEM((n_slab, rows, LANES), _F32),
            pltpu.VMEM((N_CLUSTERS, rows, 2 * CLUSTER_STATE), _F32),
            pltpu.VMEM((N_CLUSTERS, 2, batch, CLUSTER_STATE), _F32),
        ],
        compiler_params=pltpu.CompilerParams(
            dimension_semantics=("arbitrary",), vmem_limit_bytes=VMEM_LIMIT),
        name="mixer",
    )(u_tm, s5w["lam"], s5w["bbd"], s5w["cbd"], s5w["d_skip"],
      sinks, q_tm, k_tm, k_tm, v_tm, v_tm, g_attn)


def _in_proj(h, g_mix, w_in_ref, u_ref, q_ref, k_ref, v_ref):
    hn = _rms(h, g_mix).astype(_BF16)
    uq_width = SSM_WIDTH + ATTN_WIDTH
    uq = _dot(hn, w_in_ref[:, :uq_width])
    u_ref[...] = uq[:, :SSM_WIDTH].astype(u_ref.dtype)
    q_ref[...] = (uq[:, SSM_WIDTH:] * Q_SCALE).astype(q_ref.dtype)
    half = hn.shape[0] // 2
    for rows in (slice(0, half), slice(half, 2 * half)):
        kv = _dot(hn[rows, :], w_in_ref[:, uq_width:])
        k_ref[rows, :] = kv[:, :KV_WIDTH].astype(k_ref.dtype)
        v_ref[rows, :] = kv[:, KV_WIDTH:].astype(v_ref.dtype)


def _pre_kernel(h_ref, g_mix_ref, w_in_ref, u_ref, q_ref, k_ref, v_ref):
    _in_proj(h_ref[...], g_mix_ref[...], w_in_ref, u_ref, q_ref, k_ref, v_ref)


def _mid_kernel(*refs, last):
    (h_ref, ssm_ref, attn_ref, p_ref, w_out_ref, g_ffn_ref, w_fi_ref, w_fo_ref, g_ple_ref,
     w_pg_ref, w_pp_ref, w_glu_ref, g_ssm_ref) = refs[:13]
    if last:
        g_fin_ref, out_ref = refs[13:]
    else:
        g_mix_ref, w_in_ref, h_out_ref, u_ref, q_ref, k_ref, v_ref = refs[13:]

    y = _gelu_tanh(ssm_ref[...].astype(_F32))
    y = y * _sigmoid(_dot(y.astype(_BF16), w_glu_ref[...]))
    ssm = _rms(y, g_ssm_ref[...]).astype(_BF16)
    res_ref = out_ref if last else h_out_ref
    res_ref[...] = (h_ref[...] + _dot(ssm, w_out_ref[:SSM_WIDTH, :])
                    + _dot(attn_ref[...], w_out_ref[SSM_WIDTH:, :]))
    hn = _rms(res_ref[...], g_ffn_ref[...]).astype(_BF16)
    for c in range(FFN_HIDDEN // FFN_CHUNK):
        lo = c * FFN_CHUNK
        gate = _dot(hn, w_fi_ref[:, lo:lo + FFN_CHUNK])
        up = _dot(hn, w_fi_ref[:, FFN_HIDDEN + lo:FFN_HIDDEN + lo + FFN_CHUNK])
        act = (gate * _sigmoid(gate) * up).astype(_BF16)
        res_ref[...] += _dot(act, w_fo_ref[lo:lo + FFN_CHUNK, :])
    h = res_ref[...]
    gate = _sigmoid(_dot(_rms(h, g_ple_ref[...]).astype(_BF16), w_pg_ref[...]))
    h = h + gate * _dot(p_ref[...].astype(_BF16), w_pp_ref[...])
    if last:
        out_ref[...] = _rms(h, g_fin_ref[...])
    else:
        h_out_ref[...] = h
        _in_proj(h, g_mix_ref[...], w_in_ref, u_ref, q_ref, k_ref, v_ref)


def _layer_const(layer, shape):
    return pl.BlockSpec((None,) + shape, lambda b, i: (layer,) + (0,) * len(shape),
                        pipeline_mode=pl.Buffered(1))


def _proj_out_specs(tile, batch, seq):
    tm = lambda w: pl.BlockSpec((tile, w), lambda b, i: (i, b))
    specs = [tm(SSM_WIDTH), tm(ATTN_WIDTH), tm(KV_WIDTH), tm(KV_WIDTH)]
    shapes = [jax.ShapeDtypeStruct((seq, batch * w), _BF16)
              for w in (SSM_WIDTH, ATTN_WIDTH, KV_WIDTH, KV_WIDTH)]
    return specs, shapes


def _pre(x2d, g_mix, w_in, *, batch, seq):
    tile = PRE_TILE
    n_t = seq // tile
    specs, shapes = _proj_out_specs(tile, batch, seq)
    return pl.pallas_call(
        _pre_kernel,
        grid=(batch, n_t),
        in_specs=[
            pl.BlockSpec((tile, D_MODEL), lambda b, i: (b * n_t + i, 0)),
            _layer_const(0, (1, D_MODEL)),
            _layer_const(0, (D_MODEL, IN_WIDTH)),
        ],
        out_specs=specs,
        out_shape=shapes,
        compiler_params=pltpu.CompilerParams(
            dimension_semantics=("parallel", "parallel"), vmem_limit_bytes=VMEM_LIMIT),
        name="pre_in_proj",
    )(x2d, g_mix, w_in)


def _mid(h2d, ssm_tm, attn_tm, p2d, weights, layer, *, batch, seq, depth, h_batch_major):
    tile = TOKEN_TILE
    n_t = seq // tile
    last = layer == depth - 1
    bm = lambda w, off=0: pl.BlockSpec((tile, w), lambda b, i: (off + b * n_t + i, 0))
    tm = lambda w: pl.BlockSpec((tile, w), lambda b, i: (i, b))
    in_specs = [
        bm(D_MODEL) if h_batch_major else tm(D_MODEL),
        tm(SSM_WIDTH), tm(ATTN_WIDTH),
        bm(PLE_DIM, layer * batch * n_t),
        _layer_const(layer, (D_MODEL, D_MODEL)),
        _layer_const(layer, (1, D_MODEL)),
        _layer_const(layer, (D_MODEL, 2 * FFN_HIDDEN)),
        _layer_const(layer, (FFN_HIDDEN, D_MODEL)),
        _layer_const(layer, (1, D_MODEL)),
        _layer_const(layer, (D_MODEL, D_MODEL)),
        _layer_const(layer, (PLE_DIM, D_MODEL)),
        _layer_const(layer, (SSM_WIDTH, SSM_WIDTH)),
        _layer_const(layer, (1, SSM_WIDTH)),
    ]
    args = [h2d, ssm_tm, attn_tm, p2d, weights["w_out"], weights["g_ffn"], weights["w_fi"],
            weights["w_fo"], weights["g_ple"], weights["w_pg"], weights["w_pp"],
            weights["w_glu"], weights["g_ssm"]]
    if last:
        in_specs.append(pl.BlockSpec((1, D_MODEL), lambda b, i: (0, 0)))
        args.append(weights["g_fin"])
        out_specs = [bm(D_MODEL)]
        out_shape = [jax.ShapeDtypeStruct((batch * seq, D_MODEL), _F32)]
    else:
        in_specs += [_layer_const(layer + 1, (1, D_MODEL)),
                     _layer_const(layer + 1, (D_MODEL, IN_WIDTH))]
        args += [weights["g_mix"], weights["w_in"]]
        specs, shapes = _proj_out_specs(tile, batch, seq)
        out_specs = [tm(D_MODEL)] + specs
        out_shape = [jax.ShapeDtypeStruct((seq, batch * D_MODEL), _F32)] + shapes
    return pl.pallas_call(
        functools.partial(_mid_kernel, last=last),
        grid=(batch, n_t),
        in_specs=in_specs,
        out_specs=out_specs,
        out_shape=out_shape,
        compiler_params=pltpu.CompilerParams(
            dimension_semantics=("parallel", "parallel"), vmem_limit_bytes=VMEM_LIMIT),
        name="mid_last" if last else "mid",
    )(*args)


def _pair_heads(a, axis):
    gq = N_Q_HEADS // N_KV_HEADS
    shape = a.shape
    a = a.reshape(shape[:axis] + (N_KV_HEADS, gq, HEAD_DIM) + shape[axis + 1:])
    return jnp.swapaxes(a, axis, axis + 1).reshape(shape)


def kernel(x, p, norm_mix, w_in, ssm_a_re, ssm_a_im, ssm_log_dt, ssm_b_re, ssm_b_im, ssm_c_re,
           ssm_c_im, ssm_d, ssm_w_glu, attn_sinks, norm_ssm_out, norm_attn_out, w_out, norm_ffn,
           w_ffn_in, w_ffn_out, norm_ple, w_ple_gate, w_ple_proj, norm_final):
    batch, seq, _ = x.shape
    depth = w_in.shape[0]
    assert batch == SUBLANES and seq % max(TOKEN_TILE, PRE_TILE, S5_CHUNK * batch) == 0

    row = lambda a: a.reshape(a.shape[0], 1, a.shape[-1])
    q_lo, q_hi = SSM_WIDTH, SSM_WIDTH + ATTN_WIDTH
    weights = {
        "g_mix": row(norm_mix),
        "w_in": jnp.concatenate(
            [w_in[:, :, :q_lo], _pair_heads(w_in[:, :, q_lo:q_hi], 2), w_in[:, :, q_hi:]],
            axis=2).astype(_BF16),
        "w_out": jnp.concatenate(
            [w_out[:, :SSM_WIDTH, :], _pair_heads(w_out[:, SSM_WIDTH:, :], 1)],
            axis=1).astype(_BF16),
        "g_ffn": row(norm_ffn),
        "w_fi": w_ffn_in.astype(_BF16),
        "w_fo": w_ffn_out.astype(_BF16),
        "g_ple": row(norm_ple),
        "w_pg": w_ple_gate.astype(_BF16),
        "w_pp": w_ple_proj.astype(_BF16),
        "g_fin": norm_final.reshape(1, D_MODEL),
    }
    g_attn = row(_pair_heads(norm_attn_out, 1))
    sinks = row(attn_sinks)

    lam, bbd, cbd = _s5_prep(ssm_a_re, ssm_a_im, ssm_log_dt, ssm_b_re, ssm_b_im, ssm_c_re, ssm_c_im)
    s5w = {"lam": lam, "bbd": bbd, "cbd": cbd, "d_skip": row(ssm_d)}
    weights["w_glu"] = ssm_w_glu.astype(_BF16)
    weights["g_ssm"] = row(norm_ssm_out)

    p2d = p.reshape(depth * batch * seq, PLE_DIM)
    h2d = x.reshape(batch * seq, D_MODEL)
    u, q, k, v = _pre(h2d, weights["g_mix"], weights["w_in"], batch=batch, seq=seq)
    for layer in range(depth):
        ssm, attn = _mixer(u, q, k, v, s5w, sinks, g_attn, layer, batch=batch, seq=seq)
        outs = _mid(h2d, ssm, attn, p2d, weights, layer,
                    batch=batch, seq=seq, depth=depth, h_batch_major=(layer == 0))
        if layer == depth - 1:
            return outs[0].reshape(batch, seq, D_MODEL)
        h2d, u, q, k, v = outs
```

```python
import functools
import math

import jax
import jax.numpy as jnp
from jax import lax
from jax.experimental import pallas as pl
from jax.experimental.pallas import tpu as pltpu

D_MODEL = 1024
SSM_WIDTH = 512
SSM_GROUP_WIDTH = 16
N_SSM_GROUPS = 32
SSM_STATE = 64
HEAD_DIM = 64
N_Q_HEADS = 8
N_KV_HEADS = 2
ATTN_WIDTH = 512
KV_WIDTH = 128
IN_WIDTH = 1280
WINDOW = 128
BLOCK = 128
FFN_HIDDEN = 2816
PLE_DIM = 256
RMS_EPS = 1e-6

SUBLANES = 8
LANES = 128
MXU_DIM = 256

CLUSTER_GROUPS = MXU_DIM // SSM_GROUP_WIDTH
N_CLUSTERS = N_SSM_GROUPS // CLUSTER_GROUPS
CLUSTER_IN = CLUSTER_GROUPS * SSM_GROUP_WIDTH
CLUSTER_STATE = CLUSTER_GROUPS * SSM_STATE

TOKEN_TILE = 1024
PRE_TILE = 1024
S5_CHUNK = 64
FFN_CHUNK = 256
VMEM_LIMIT = 60 * 1024 * 1024
LOG2_E = math.log2(math.e)
Q_SCALE = HEAD_DIM ** -0.5 * LOG2_E

_F32 = jnp.float32
_BF16 = jnp.bfloat16


def _dot(a, b):
    return jnp.dot(a, b, preferred_element_type=_F32)


def _rms(x, g):
    ms = jnp.mean(x * x, axis=-1, keepdims=True)
    return x * lax.rsqrt(ms + RMS_EPS) * g


def _sigmoid(x):
    return 1.0 / (1.0 + jnp.exp2(x * (-LOG2_E)))


def _gelu_tanh(x):
    k = -2.0 * LOG2_E * math.sqrt(2.0 / math.pi)
    t = x * (x * x * (k * 0.044715) + k)
    return x / (1.0 + jnp.exp2(t))


def _s5_prep_kernel(ar_ref, ai_ref, ldt_ref, btr_ref, bti_ref, cr_ref, ci_ref,
                    lbr_ref, lbi_ref, bbd_ref, cbd_ref):
    G, H, P = N_SSM_GROUPS, SSM_GROUP_WIDTH, SSM_STATE
    ar = ar_ref[...]
    ai = ai_ref[...]
    dt = jnp.exp(ldt_ref[...])
    mag = jnp.exp(ar * dt)
    lb_re = mag * jnp.cos(ai * dt)
    lb_im = mag * jnp.sin(ai * dt)
    den = ar * ar + ai * ai
    nr = lb_re - 1.0
    ni = lb_im
    f_re = (nr * ar + ni * ai) / den
    f_im = (ni * ar - nr * ai) / den
    lbr_ref[...] = lb_re
    lbi_ref[...] = lb_im
    cr = cr_ref[...]
    ci = ci_ref[...]
    fr = f_re[:, None, :]
    fi = f_im[:, None, :]
    cf_re = (cr * fr - ci * fi).reshape(G * H, P)
    cf_im_neg = (-(cr * fi + ci * fr)).reshape(G * H, P)

    spread = (lax.broadcasted_iota(jnp.int32, (P, CLUSTER_STATE), 1) % P
              == lax.broadcasted_iota(jnp.int32, (P, CLUSTER_STATE), 0)).astype(_BF16)
    own_group = (lax.broadcasted_iota(jnp.int32, (CLUSTER_IN, CLUSTER_STATE), 0) // H
                 == lax.broadcasted_iota(jnp.int32, (CLUSTER_IN, CLUSTER_STATE), 1) // P)

    def block_diag(rows_gh_p):
        return jnp.where(own_group, _dot(rows_gh_p.astype(_BF16), spread), 0.0)

    for c in range(N_CLUSTERS):
        rows = slice(c * CLUSTER_IN, (c + 1) * CLUSTER_IN)
        for part, (bt_ref, cf) in enumerate(((btr_ref, cf_re), (bti_ref, cf_im_neg))):
            cols = slice(part * CLUSTER_STATE, (part + 1) * CLUSTER_STATE)
            bbd_ref[c, :, cols] = block_diag(bt_ref[rows, :]).astype(bbd_ref.dtype)
            cbd_ref[c, cols, :] = block_diag(cf[rows, :]).T.astype(cbd_ref.dtype)


def _s5_prep(a_re, a_im, log_dt, b_re, b_im, c_re, c_im):
    depth = a_re.shape[0]
    G, P, H, C = N_SSM_GROUPS, SSM_STATE, SSM_GROUP_WIDTH, N_CLUSTERS
    gp = pl.BlockSpec((None, G, P), lambda i: (i, 0, 0))
    g1 = pl.BlockSpec((None, G, 1), lambda i: (i, 0, 0))
    gh_p = pl.BlockSpec((None, G * H, P), lambda i: (i, 0, 0))
    ghp = pl.BlockSpec((None, G, H, P), lambda i: (i, 0, 0, 0))
    rows_gh = lambda b: jnp.swapaxes(b, 2, 3).reshape(depth, G * H, P)
    lb_re, lb_im, bbd, cbd = pl.pallas_call(
        _s5_prep_kernel,
        grid=(depth,),
        in_specs=[gp, gp, g1, gh_p, gh_p, ghp, ghp],
        out_specs=[gp, gp,
                   pl.BlockSpec((None, C, CLUSTER_IN, 2 * CLUSTER_STATE), lambda i: (i, 0, 0, 0)),
                   pl.BlockSpec((None, C, 2 * CLUSTER_STATE, CLUSTER_IN), lambda i: (i, 0, 0, 0))],
        out_shape=[jax.ShapeDtypeStruct((depth, G, P), _F32)] * 2
        + [jax.ShapeDtypeStruct((depth, C, CLUSTER_IN, 2 * CLUSTER_STATE), _BF16),
           jax.ShapeDtypeStruct((depth, C, 2 * CLUSTER_STATE, CLUSTER_IN), _BF16)],
        name="s5_prep",
    )(a_re, a_im, log_dt.reshape(depth, G, 1), rows_gh(b_re), rows_gh(b_im), c_re, c_im)
    lam = jnp.stack([lb_re.reshape(depth, C, CLUSTER_STATE), lb_im.reshape(depth, C, CLUSTER_STATE)],
                    axis=2)
    return lam, bbd, cbd


SCAN_SLICES = 4


def _s5_phases(first_step, u_ref, lam_ref, bbd_ref, cbd_ref, d_ref, o_ref,
               us_ref, ys_ref, bu_ref, st_ref, *, chunk, batch):
    n_slab = SSM_WIDTH // LANES
    slabs_per_cluster = CLUSTER_IN // LANES

    @pl.when(first_step)
    def _():
        st_ref[...] = jnp.zeros_like(st_ref)

    for b in range(batch):
        for j in range(n_slab):
            lo = b * SSM_WIDTH + j * LANES
            us_ref[j, pl.ds(b, chunk, stride=batch), :] = u_ref[:, lo:lo + LANES].astype(_F32)
    for c in range(N_CLUSTERS):
        u_c = jnp.concatenate(
            [us_ref[c * slabs_per_cluster + j] for j in range(slabs_per_cluster)], axis=1)
        bu_ref[c] = _dot(u_c.astype(_BF16), bbd_ref[c])
    yield

    ys = []
    for c in range(N_CLUSTERS):
        lr = jnp.broadcast_to(lam_ref[c, 0:1, :], (batch, CLUSTER_STATE))
        li = jnp.broadcast_to(lam_ref[c, 1:2, :], (batch, CLUSTER_STATE))
        sr, si = st_ref[c, 0], st_ref[c, 1]
        for t in range(chunk):
            r0 = t * batch
            br = bu_ref[c, r0:r0 + batch, 0:CLUSTER_STATE]
            bi = bu_ref[c, r0:r0 + batch, CLUSTER_STATE:2 * CLUSTER_STATE]
            sr, si = lr * sr - li * si + br, lr * si + li * sr + bi
            bu_ref[c, r0:r0 + batch, 0:CLUSTER_STATE] = sr
            bu_ref[c, r0:r0 + batch, CLUSTER_STATE:2 * CLUSTER_STATE] = si
            if (t + 1) % (chunk // SCAN_SLICES) == 0:
                yield
        st_ref[c, 0] = sr
        st_ref[c, 1] = si
        ys.append(_dot(bu_ref[c].astype(_BF16), cbd_ref[c]))
        yield
    u_all = jnp.concatenate([us_ref[j] for j in range(n_slab)], axis=1)
    y = jnp.concatenate(ys, axis=1) + d_ref[...] * u_all
    yield
    for j in range(n_slab):
        ys_ref[j] = y[:, j * LANES:(j + 1) * LANES]
    for b in range(batch):
        for j in range(n_slab):
            lo = b * SSM_WIDTH + j * LANES
            o_ref[:, lo:lo + LANES] = ys_ref[j, pl.ds(b, chunk, stride=batch), :].astype(o_ref.dtype)


def _attn_phases(first_tile, sink_ref, q_ref, kc_ref, kp_ref, vc_ref, vp_ref, g_ref, o_ref, *,
                 n_blocks):
    gq = N_Q_HEADS // N_KV_HEADS
    rows = gq * BLOCK
    qpos = lax.broadcasted_iota(jnp.int32, (rows, BLOCK), 0) % BLOCK
    kidx = lax.broadcasted_iota(jnp.int32, (rows, BLOCK), 1)
    own = kidx <= qpos
    head_of_row = lax.broadcasted_iota(jnp.int32, (rows, 1), 0) // BLOCK
    low_kv = lax.broadcasted_iota(jnp.int32, (2 * BLOCK, KV_WIDTH), 1) < HEAD_DIM
    low_out = lax.broadcasted_iota(jnp.int32, (rows, LANES), 1) < HEAD_DIM

    sink_cols = []
    for kv in range(N_KV_HEADS):
        col = jnp.full((rows, 1), sink_ref[0, kv * gq], _F32)
        for j in range(1, gq):
            col = jnp.where(head_of_row == j, sink_ref[0, kv * gq + j], col)
        sink_cols.append(col * LOG2_E)

    for blk in range(n_blocks):
        lo, hi = blk * BLOCK, (blk + 1) * BLOCK
        if blk == 0:
            k_prev, v_prev = kp_ref[...], vp_ref[...]
        else:
            k_prev, v_prev = kc_ref[lo - BLOCK:lo, :], vc_ref[lo - BLOCK:lo, :]
        keys = jnp.concatenate([k_prev, kc_ref[lo:hi, :]], axis=0)
        vals = jnp.concatenate([v_prev, vc_ref[lo:hi, :]], axis=0)
        zero = jnp.zeros_like(keys)
        q_all = jnp.concatenate([q_ref[lo:hi, j * LANES:(j + 1) * LANES] for j in range(gq)], axis=0)
        acc = None
        invs = []
        for kv in range(N_KV_HEADS):
            sel = low_kv if kv == 0 else jnp.logical_not(low_kv)
            k_sel = jnp.where(sel, keys, zero)
            v_sel = jnp.where(sel, vals, zero)
            s = lax.dot_general(q_all, k_sel, (((1,), (1,)), ((), ())),
                                preferred_element_type=_F32)
            s_prev = s[:, :BLOCK]
            if blk == 0:
                s_prev = jnp.where(first_tile, -jnp.inf, s_prev)
            s = jnp.where(own, s[:, BLOCK:], s_prev)
            sink = sink_cols[kv]
            m = jnp.maximum(jnp.max(s, axis=1, keepdims=True), sink)
            e = jnp.exp2(s - m)
            den = jnp.sum(e, axis=1, keepdims=True) + jnp.exp2(sink - m)
            invs.append(1.0 / den)
            p = jnp.concatenate([jnp.where(own, 0.0, e), jnp.where(own, e, 0.0)], axis=1)
            pv = _dot(p.astype(_BF16), v_sel)
            acc = pv if acc is None else acc + pv
        out = acc * jnp.where(low_out, invs[0], invs[1])
        for j in range(gq):
            o_ref[lo:hi, j * LANES:(j + 1) * LANES] = (
                out[j * BLOCK:(j + 1) * BLOCK, :].astype(o_ref.dtype))
        yield


N_S5_IN = 5
N_ATTN_IN = 7
ATTN_AFTER_S5_PHASE = (0, 3, 6, 10)


def _mixer_kernel(*refs, chunk, batch, n_blocks, tiles_per_seq):
    s5_in = refs[:N_S5_IN]
    attn_in = refs[N_S5_IN:N_S5_IN + N_ATTN_IN]
    ssm_ref, attn_ref = refs[N_S5_IN + N_ATTN_IN:N_S5_IN + N_ATTN_IN + 2]
    scratch = refs[N_S5_IN + N_ATTN_IN + 2:]
    step = pl.program_id(0)
    s5 = _s5_phases(step == 0, *s5_in, ssm_ref, *scratch, chunk=chunk, batch=batch)
    attn = _attn_phases(step % tiles_per_seq == 0, *attn_in, attn_ref, n_blocks=n_blocks)
    for phase, _ in enumerate(s5):
        if phase in ATTN_AFTER_S5_PHASE:
            next(attn)
    for _ in attn:
        pass


def _mixer(u_tm, q_tm, k_tm, v_tm, s5w, sinks, g_attn, layer, *, batch, seq):
    chunk = S5_CHUNK
    rows = chunk * batch
    tile = rows
    n_blocks = tile // BLOCK
    tiles_per_seq = seq // tile
    n_slab = SSM_WIDTH // LANES
    const = lambda shape: pl.BlockSpec((None,) + shape, lambda i: (layer,) + (0,) * len(shape))
    cur = lambda w: pl.BlockSpec((tile, w), lambda i: (i % tiles_per_seq, i // tiles_per_seq))
    prev = lambda w: pl.BlockSpec(
        (BLOCK, w),
        lambda i: (jnp.maximum((i % tiles_per_seq) * n_blocks - 1, 0), i // tiles_per_seq))
    return pl.pallas_call(
        functools.partial(_mixer_kernel, chunk=chunk, batch=batch, n_blocks=n_blocks,
                          tiles_per_seq=tiles_per_seq),
        grid=(seq // chunk,),
        in_specs=[
            pl.BlockSpec((chunk, batch * SSM_WIDTH), lambda i: (i, 0)),
            const((N_CLUSTERS, 2, CLUSTER_STATE)),
            const((N_CLUSTERS, CLUSTER_IN, 2 * CLUSTER_STATE)),
            const((N_CLUSTERS, 2 * CLUSTER_STATE, CLUSTER_IN)),
            const((1, SSM_WIDTH)),
            pl.BlockSpec((None, 1, N_Q_HEADS), lambda i: (layer, 0, 0), memory_space=pltpu.SMEM),
            cur(ATTN_WIDTH), cur(KV_WIDTH), prev(KV_WIDTH), cur(KV_WIDTH), prev(KV_WIDTH),
            const((1, ATTN_WIDTH)),
        ],
        out_specs=[pl.BlockSpec((chunk, batch * SSM_WIDTH), lambda i: (i, 0)), cur(ATTN_WIDTH)],
        out_shape=[jax.ShapeDtypeStruct((seq, batch * SSM_WIDTH), _BF16),
                   jax.ShapeDtypeStruct((seq, batch * ATTN_WIDTH), _BF16)],
        scratch_shapes=[
            pltpu.VMEM((n_slab, rows, LANES), _F32),
            pltpu.VMEM((n_slab, rows, LANES), _F32),
            pltpu.VMEM((N_CLUSTERS, rows, 2 * CLUSTER_STATE), _F32),
            pltpu.VMEM((N_CLUSTERS, 2, batch, CLUSTER_STATE), _F32),
        ],
        compiler_params=pltpu.CompilerParams(
            dimension_semantics=("arbitrary",), vmem_limit_bytes=VMEM_LIMIT),
        name="mixer",
    )(u_tm, s5w["lam"], s5w["bbd"], s5w["cbd"], s5w["d_skip"],
      sinks, q_tm, k_tm, k_tm, v_tm, v_tm, g_attn)


def _in_proj(h, g_mix, w_in_ref, u_ref, q_ref, k_ref, v_ref):
    hn = _rms(h, g_mix).astype(_BF16)
    uq_width = SSM_WIDTH + ATTN_WIDTH
    uq = _dot(hn, w_in_ref[:, :uq_width])
    u_ref[...] = uq[:, :SSM_WIDTH].astype(u_ref.dtype)
    q_ref[...] = (uq[:, SSM_WIDTH:] * Q_SCALE).astype(q_ref.dtype)
    half = hn.shape[0] // 2
    for rows in (slice(0, half), slice(half, 2 * half)):
        kv = _dot(hn[rows, :], w_in_ref[:, uq_width:])
        k_ref[rows, :] = kv[:, :KV_WIDTH].astype(k_ref.dtype)
        v_ref[rows, :] = kv[:, KV_WIDTH:].astype(v_ref.dtype)


def _pre_kernel(h_ref, g_mix_ref, w_in_ref, u_ref, q_ref, k_ref, v_ref):
    _in_proj(h_ref[...], g_mix_ref[...], w_in_ref, u_ref, q_ref, k_ref, v_ref)


def _mid_kernel(*refs, last):
    (h_ref, ssm_ref, attn_ref, p_ref, w_out_ref, g_ffn_ref, w_fi_ref, w_fo_ref, g_ple_ref,
     w_pg_ref, w_pp_ref, w_glu_ref, g_ssm_ref, g_attn_ref) = refs[:14]
    if last:
        g_fin_ref, out_ref = refs[14:]
    else:
        g_mix_ref, w_in_ref, h_out_ref, u_ref, q_ref, k_ref, v_ref = refs[14:]

    y = _gelu_tanh(ssm_ref[...].astype(_F32))
    y = y * _sigmoid(_dot(y.astype(_BF16), w_glu_ref[...]))
    ssm = _rms(y, g_ssm_ref[...]).astype(_BF16)
    res_ref = out_ref if last else h_out_ref
    attn = _rms(attn_ref[...].astype(_F32), g_attn_ref[...]).astype(_BF16)
    res_ref[...] = (h_ref[...] + _dot(ssm, w_out_ref[:SSM_WIDTH, :])
                    + _dot(attn, w_out_ref[SSM_WIDTH:, :]))
    hn = _rms(res_ref[...], g_ffn_ref[...]).astype(_BF16)
    for c in range(FFN_HIDDEN // FFN_CHUNK):
        lo = c * FFN_CHUNK
        gate = _dot(hn, w_fi_ref[:, lo:lo + FFN_CHUNK])
        up = _dot(hn, w_fi_ref[:, FFN_HIDDEN + lo:FFN_HIDDEN + lo + FFN_CHUNK])
        act = (gate * _sigmoid(gate) * up).astype(_BF16)
        res_ref[...] += _dot(act, w_fo_ref[lo:lo + FFN_CHUNK, :])
    h = res_ref[...]
    gate = _sigmoid(_dot(_rms(h, g_ple_ref[...]).astype(_BF16), w_pg_ref[...]))
    h = h + gate * _dot(p_ref[...].astype(_BF16), w_pp_ref[...])
    if last:
        out_ref[...] = _rms(h, g_fin_ref[...])
    else:
        h_out_ref[...] = h
        _in_proj(h, g_mix_ref[...], w_in_ref, u_ref, q_ref, k_ref, v_ref)


def _layer_const(layer, shape):
    return pl.BlockSpec((None,) + shape, lambda b, i: (layer,) + (0,) * len(shape),
                        pipeline_mode=pl.Buffered(1))


def _proj_out_specs(tile, batch, seq):
    tm = lambda w: pl.BlockSpec((tile, w), lambda b, i: (i, b))
    specs = [tm(SSM_WIDTH), tm(ATTN_WIDTH), tm(KV_WIDTH), tm(KV_WIDTH)]
    shapes = [jax.ShapeDtypeStruct((seq, batch * w), _BF16)
              for w in (SSM_WIDTH, ATTN_WIDTH, KV_WIDTH, KV_WIDTH)]
    return specs, shapes


def _pre(x2d, g_mix, w_in, *, batch, seq):
    tile = PRE_TILE
    n_t = seq // tile
    specs, shapes = _proj_out_specs(tile, batch, seq)
    return pl.pallas_call(
        _pre_kernel,
        grid=(batch, n_t),
        in_specs=[
            pl.BlockSpec((tile, D_MODEL), lambda b, i: (b * n_t + i, 0)),
            _layer_const(0, (1, D_MODEL)),
            _layer_const(0, (D_MODEL, IN_WIDTH)),
        ],
        out_specs=specs,
        out_shape=shapes,
        compiler_params=pltpu.CompilerParams(
            dimension_semantics=("parallel", "parallel"), vmem_limit_bytes=VMEM_LIMIT),
        name="pre_in_proj",
    )(x2d, g_mix, w_in)


def _mid(h2d, ssm_tm, attn_tm, p2d, weights, layer, *, batch, seq, depth, h_batch_major):
    tile = TOKEN_TILE
    n_t = seq // tile
    last = layer == depth - 1
    bm = lambda w, off=0: pl.BlockSpec((tile, w), lambda b, i: (off + b * n_t + i, 0))
    tm = lambda w: pl.BlockSpec((tile, w), lambda b, i: (i, b))
    in_specs = [
        bm(D_MODEL) if h_batch_major else tm(D_MODEL),
        tm(SSM_WIDTH), tm(ATTN_WIDTH),
        bm(PLE_DIM, layer * batch * n_t),
        _layer_const(layer, (D_MODEL, D_MODEL)),
        _layer_const(layer, (1, D_MODEL)),
        _layer_const(layer, (D_MODEL, 2 * FFN_HIDDEN)),
        _layer_const(layer, (FFN_HIDDEN, D_MODEL)),
        _layer_const(layer, (1, D_MODEL)),
        _layer_const(layer, (D_MODEL, D_MODEL)),
        _layer_const(layer, (PLE_DIM, D_MODEL)),
        _layer_const(layer, (SSM_WIDTH, SSM_WIDTH)),
        _layer_const(layer, (1, SSM_WIDTH)),
        _layer_const(layer, (1, ATTN_WIDTH)),
    ]
    args = [h2d, ssm_tm, attn_tm, p2d, weights["w_out"], weights["g_ffn"], weights["w_fi"],
            weights["w_fo"], weights["g_ple"], weights["w_pg"], weights["w_pp"],
            weights["w_glu"], weights["g_ssm"], weights["g_attn"]]
    if last:
        in_specs.append(pl.BlockSpec((1, D_MODEL), lambda b, i: (0, 0)))
        args.append(weights["g_fin"])
        out_specs = [bm(D_MODEL)]
        out_shape = [jax.ShapeDtypeStruct((batch * seq, D_MODEL), _F32)]
    else:
        in_specs += [_layer_const(layer + 1, (1, D_MODEL)),
                     _layer_const(layer + 1, (D_MODEL, IN_WIDTH))]
        args += [weights["g_mix"], weights["w_in"]]
        specs, shapes = _proj_out_specs(tile, batch, seq)
        out_specs = [tm(D_MODEL)] + specs
        out_shape = [jax.ShapeDtypeStruct((seq, batch * D_MODEL), _F32)] + shapes
    return pl.pallas_call(
        functools.partial(_mid_kernel, last=last),
        grid=(batch, n_t),
        in_specs=in_specs,
        out_specs=out_specs,
        out_shape=out_shape,
        compiler_params=pltpu.CompilerParams(
            dimension_semantics=("parallel", "parallel"), vmem_limit_bytes=VMEM_LIMIT),
        name="mid_last" if last else "mid",
    )(*args)


def _pair_heads(a, axis):
    gq = N_Q_HEADS // N_KV_HEADS
    shape = a.shape
    a = a.reshape(shape[:axis] + (N_KV_HEADS, gq, HEAD_DIM) + shape[axis + 1:])
    return jnp.swapaxes(a, axis, axis + 1).reshape(shape)


def kernel(x, p, norm_mix, w_in, ssm_a_re, ssm_a_im, ssm_log_dt, ssm_b_re, ssm_b_im, ssm_c_re,
           ssm_c_im, ssm_d, ssm_w_glu, attn_sinks, norm_ssm_out, norm_attn_out, w_out, norm_ffn,
           w_ffn_in, w_ffn_out, norm_ple, w_ple_gate, w_ple_proj, norm_final):
    batch, seq, _ = x.shape
    depth = w_in.shape[0]
    assert batch == SUBLANES and seq % max(TOKEN_TILE, PRE_TILE, S5_CHUNK * batch) == 0

    row = lambda a: a.reshape(a.shape[0], 1, a.shape[-1])
    q_lo, q_hi = SSM_WIDTH, SSM_WIDTH + ATTN_WIDTH
    weights = {
        "g_mix": row(norm_mix),
        "w_in": jnp.concatenate(
            [w_in[:, :, :q_lo], _pair_heads(w_in[:, :, q_lo:q_hi], 2), w_in[:, :, q_hi:]],
            axis=2).astype(_BF16),
        "w_out": jnp.concatenate(
            [w_out[:, :SSM_WIDTH, :], _pair_heads(w_out[:, SSM_WIDTH:, :], 1)],
            axis=1).astype(_BF16),
        "g_ffn": row(norm_ffn),
        "w_fi": w_ffn_in.astype(_BF16),
        "w_fo": w_ffn_out.astype(_BF16),
        "g_ple": row(norm_ple),
        "w_pg": w_ple_gate.astype(_BF16),
        "w_pp": w_ple_proj.astype(_BF16),
        "g_fin": norm_final.reshape(1, D_MODEL),
    }
    g_attn = row(_pair_heads(norm_attn_out, 1))
    sinks = row(attn_sinks)

    lam, bbd, cbd = _s5_prep(ssm_a_re, ssm_a_im, ssm_log_dt, ssm_b_re, ssm_b_im, ssm_c_re, ssm_c_im)
    s5w = {"lam": lam, "bbd": bbd, "cbd": cbd, "d_skip": row(ssm_d)}
    weights["w_glu"] = ssm_w_glu.astype(_BF16)
    weights["g_ssm"] = row(norm_ssm_out)
    weights["g_attn"] = g_attn

    p2d = p.reshape(depth * batch * seq, PLE_DIM)
    h2d = x.reshape(batch * seq, D_MODEL)
    u, q, k, v = _pre(h2d, weights["g_mix"], weights["w_in"], batch=batch, seq=seq)
    for layer in range(depth):
        ssm, attn = _mixer(u, q, k, v, s5w, sinks, g_attn, layer, batch=batch, seq=seq)
        outs = _mid(h2d, ssm, attn, p2d, weights, layer,
                    batch=batch, seq=seq, depth=depth, h_batch_major=(layer == 0))
        if layer == depth - 1:
            return outs[0].reshape(batch, seq, D_MODEL)
        h2d, u, q, k, v = outs
```
